```python
import math
import jax, jax.numpy as jnp
from jax import lax
import numpy as np

D_MODEL = 4096
BATCH = 4
SEQ = 2048
DEPTH = 4
DEC_BATCH = 32
DEC_SEQ = 8
PAST_LEN = 8192
PAGE_SIZE = 128

N_A = DEPTH // 2
N_B = DEPTH - N_A
MEM_LEN = 256
MEM_HEADS = 4
MEM_HEAD_DIM = D_MODEL // 16
D_MEM = MEM_HEADS * MEM_HEAD_DIM
D_TOK = D_MODEL - D_MEM
CHUNK = 128
A_GROUPS = 8
A_GROUP_DIM = D_TOK // A_GROUPS
WINDOW = 128
HEAD_DIM = 64
N_Q_HEADS = D_TOK // HEAD_DIM
N_KV_HEADS = N_Q_HEADS // 8
Q_PER_KV = N_Q_HEADS // N_KV_HEADS
KV_WIDTH = N_KV_HEADS * HEAD_DIM
WIN_BUF = min(WINDOW, PAST_LEN)
N_EXPERTS = 64
TOP_K = 8
N_GROUPS = 8
TOPK_GROUPS = 4
D_EXPERT = D_MODEL * 3 // 16
D_SHARED = D_EXPERT
ROUTED_SCALE = 2.5
MOE_BLOCK = 128
ALPHA = (2 * DEPTH) ** 0.25
BETA = (8 * DEPTH) ** -0.25
LN_EPS = 1e-5

kernel_name = 'yoco_gmlp_sinkwin_moe_step'


def layer_norm(x, g, b):
    xf = x.astype(jnp.float32)
    mu = xf.mean(-1, keepdims=True)
    var = jnp.mean(jnp.square(xf - mu), -1, keepdims=True)
    return ((xf - mu) * lax.rsqrt(var + LN_EPS) * g + b).astype(x.dtype)


def spatial_gate(v, w_s, b_s):
    n, t, _ = v.shape
    c = min(t, CHUNK)
    w = w_s[:, :c, :c] * jnp.tril(jnp.ones((c, c), w_s.dtype))
    vc = v.reshape(n, t // c, c, A_GROUPS, A_GROUP_DIM)
    out = jnp.einsum('gts,nksgd->nktgd', w, vc) + b_s[:, :c].T[None, None, :, :, None]
    return out.reshape(n, t, D_TOK)


def mixer_a(x, w_in, sgu_g, sgu_b, w_s, b_s):
    h = x @ w_in
    u = jax.nn.gelu(h[..., :D_TOK])
    v = layer_norm(jax.nn.gelu(h[..., D_TOK:2 * D_TOK]), sgu_g, sgu_b)
    return u * spatial_gate(v, w_s, b_s), h[..., 2 * D_TOK:], v


def sink_softmax(s, mask, sinks):
    s = jnp.where(mask, s.astype(jnp.float32) * HEAD_DIM ** -0.5, -jnp.inf)
    sink = sinks.astype(jnp.float32).reshape(N_KV_HEADS, Q_PER_KV, 1, 1)
    m = jnp.maximum(s.max(-1, keepdims=True), sink)
    p = jnp.exp(s - m)
    return p / (p.sum(-1, keepdims=True) + jnp.exp(sink - m))


def window_attn_prompt(q, k, v, sinks):
    n, t = q.shape[:2]
    nb = t // WINDOW
    qb = q.reshape(n, nb, WINDOW, N_KV_HEADS, Q_PER_KV, HEAD_DIM)

    def band(a):
        ab = a.reshape(n, nb, WINDOW, N_KV_HEADS, HEAD_DIM)
        prev = jnp.pad(ab, ((0, 0), (1, 0), (0, 0), (0, 0), (0, 0)))[:, :nb]
        return jnp.concatenate([prev, ab], axis=2)

    kb, vb = band(k), band(v)
    qpos = jnp.arange(nb)[:, None] * WINDOW + jnp.arange(WINDOW)[None]
    kpos = jnp.arange(nb)[:, None] * WINDOW - WINDOW + jnp.arange(2 * WINDOW)[None]
    diff = qpos[:, :, None] - kpos[:, None, :]
    mask = (diff >= 0) & (diff < WINDOW) & (kpos[:, None, :] >= 0)
    s = jnp.einsum('nbqhgd,nbkhd->nbhgqk', qb, kb)
    p = sink_softmax(s, mask[None, :, None, None], sinks).astype(vb.dtype)
    o = jnp.einsum('nbhgqk,nbkhd->nbqhgd', p, vb)
    return o.reshape(n, t, D_TOK)


def window_attn_sample(q, kk, vv, n_buf, sinks):
    n, l = q.shape[:2]
    kpos = jnp.concatenate([jnp.arange(n_buf) - n_buf, jnp.arange(l)])
    diff = jnp.arange(l)[:, None] - kpos[None]
    mask = (diff >= 0) & (diff < WINDOW)
    s = jnp.einsum('nqhgd,nkhd->nhgqk', q, kk)
    p = sink_softmax(s, mask, sinks).astype(vv.dtype)
    o = jnp.einsum('nhgqk,nkhd->nqhgd', p, vv)
    return o.reshape(n, l, D_TOK)


def mem_attention(q, mem_k, mem_v):
    n, t = q.shape[:2]
    s = jnp.einsum('nthd,nmhd->nhtm', q, mem_k).astype(jnp.float32) * MEM_HEAD_DIM ** -0.5
    p = jax.nn.softmax(s, axis=-1).astype(mem_v.dtype)
    return jnp.einsum('nhtm,nmhd->nthd', p, mem_v).reshape(n, t, D_MEM)


def route(xf, router_w, router_b):
    t = xf.shape[0]
    s = jax.nn.sigmoid((xf @ router_w).astype(jnp.float32))
    sb = s + router_b.astype(jnp.float32)
    gscore = lax.top_k(sb.reshape(t, N_GROUPS, N_EXPERTS // N_GROUPS), 2)[0].sum(-1)
    gidx = lax.top_k(gscore, TOPK_GROUPS)[1]
    gmask = (gidx[..., None] == jnp.arange(N_GROUPS)).any(-2)
    emask = jnp.repeat(gmask, N_EXPERTS // N_GROUPS, axis=-1)
    idx = lax.top_k(jnp.where(emask, sb, -jnp.inf), TOP_K)[1]
    w = jnp.take_along_axis(s, idx, axis=-1)
    return idx, w / w.sum(-1, keepdims=True) * ROUTED_SCALE


def moe_ffn(x, router_w, router_b, w_gu, w_dn, sh_gu, sh_dn):
    n, t, d = x.shape
    xf = x.reshape(n * t, d)
    idx, gate = route(xf, router_w, router_b)
    n_asg = n * t * TOP_K
    flat_e = idx.reshape(n_asg)
    order = jnp.argsort(flat_e)
    e_sorted = flat_e[order]
    tok_sorted = (order // TOP_K).astype(jnp.int32)
    gate_sorted = gate.reshape(n_asg)[order].astype(x.dtype)
    counts = jnp.bincount(flat_e, length=N_EXPERTS)
    padded = (counts + MOE_BLOCK - 1) // MOE_BLOCK * MOE_BLOCK
    pad_end = jnp.cumsum(padded)
    pad_start = pad_end - padded
    start = jnp.cumsum(counts) - counts
    dest = pad_start[e_sorted] + jnp.arange(n_asg) - start[e_sorted]
    n_blocks = -(-n_asg // MOE_BLOCK) + N_EXPERTS
    n_rows = n_blocks * MOE_BLOCK
    row_tok = jnp.zeros((n_rows,), jnp.int32).at[dest].set(tok_sorted)
    row_gate = jnp.zeros((n_rows,), x.dtype).at[dest].set(gate_sorted)
    block_e = jnp.minimum(jnp.searchsorted(pad_end, jnp.arange(n_blocks) * MOE_BLOCK, side='right'), N_EXPERTS - 1)

    def expert_block(acc, blk):
        toks, g, e = blk
        a, b = jnp.split(xf[toks] @ w_gu[e], 2, axis=-1)
        y = (jax.nn.silu(a) * b) @ w_dn[e]
        return acc.at[toks].add(y * g[:, None]), None

    routed, _ = lax.scan(expert_block, jnp.zeros_like(xf),
                         (row_tok.reshape(n_blocks, MOE_BLOCK), row_gate.reshape(n_blocks, MOE_BLOCK), block_e))
    sa, sb = jnp.split(xf @ sh_gu, 2, axis=-1)
    shared = (jax.nn.silu(sa) * sb) @ sh_dn
    return (routed + shared).reshape(n, t, d)


def trunk(x, mem_k, mem_v, win_k, win_v, w_in_a, sgu_g, sgu_b, w_spatial, b_spatial, w_in_b, w_kv, sinks,
          w_out, ln_g, ln_b, router_w, router_b, w_gate_up, w_down, shared_gate_up, shared_down):
    n, t, _ = x.shape
    chunk_v = []
    kk = vv = None
    for l in range(DEPTH):
        if l < N_A:
            tok, q_mem, v_rows = mixer_a(x, w_in_a[l], sgu_g[l], sgu_b[l], w_spatial[l], b_spatial[l])
            chunk_v.append(v_rows)
        else:
            if kk is None:
                kv = x @ w_kv
                k_new = kv[..., :KV_WIDTH].reshape(n, t, N_KV_HEADS, HEAD_DIM)
                v_new = kv[..., KV_WIDTH:].reshape(n, t, N_KV_HEADS, HEAD_DIM)
                if win_k is None:
                    kk, vv = k_new, v_new
                else:
                    kk = jnp.concatenate([win_k, k_new], axis=1)
                    vv = jnp.concatenate([win_v, v_new], axis=1)
            j = l - N_A
            h = x @ w_in_b[j]
            q = h[..., :D_TOK].reshape(n, t, N_KV_HEADS, Q_PER_KV, HEAD_DIM)
            q_mem = h[..., D_TOK:]
            if win_k is None:
                tok = window_attn_prompt(q, kk, vv, sinks[j])
            else:
                tok = window_attn_sample(q, kk, vv, win_k.shape[1], sinks[j])
        mo = mem_attention(q_mem.reshape(n, t, MEM_HEADS, MEM_HEAD_DIM), mem_k[l], mem_v[l])
        mix = jnp.concatenate([tok, mo], axis=-1) @ w_out[l]
        x = layer_norm(ALPHA * x + mix, ln_g[l, 0], ln_b[l, 0])
        ff = moe_ffn(x, router_w[l], router_b[l], w_gate_up[l], w_down[l], shared_gate_up[l], shared_down[l])
        x = layer_norm(ALPHA * x + ff, ln_g[l, 1], ln_b[l, 1])
    keep = win_k.shape[1] if win_k is not None else min(WINDOW, t)
    return x, chunk_v, kk[:, -keep:], vv[:, -keep:]


def setup_inputs(seed: int = 0) -> dict:
    key = jax.random.key(seed)
    ks = jax.random.split(key, 26)
    f32 = jnp.float32

    def nrm(k, shape, scale):
        return jax.random.normal(k, shape, f32) * scale

    d = D_MODEL
    kv_scale = jnp.where(jnp.arange(2 * KV_WIDTH) < KV_WIDTH, 1.0, BETA).astype(f32)
    mem_scale = jnp.where(jnp.arange(2 * D_MEM) < D_MEM, 1.0, BETA).astype(f32)
    return {
        'x_prompt': nrm(ks[0], (BATCH, SEQ, d), 1.0),
        'x_sample': nrm(ks[1], (DEC_BATCH, DEC_SEQ, d), 1.0),
        'mem_prompt': nrm(ks[2], (BATCH, MEM_LEN, d), 1.0),
        'cache_mem_k': nrm(ks[3], (DEPTH, DEC_BATCH, MEM_LEN, MEM_HEADS, MEM_HEAD_DIM), 1.0),
        'cache_mem_v': nrm(ks[4], (DEPTH, DEC_BATCH, MEM_LEN, MEM_HEADS, MEM_HEAD_DIM), BETA),
        'cache_win_k': nrm(ks[5], (DEC_BATCH, WIN_BUF, N_KV_HEADS, HEAD_DIM), 1.0),
        'cache_win_v': nrm(ks[6], (DEC_BATCH, WIN_BUF, N_KV_HEADS, HEAD_DIM), BETA),
        'w_in_a': nrm(ks[7], (N_A, d, 2 * D_TOK + D_MEM), d ** -0.5),
        'sgu_g': 1.0 + nrm(ks[8], (N_A, D_TOK), 0.05),
        'sgu_b': nrm(ks[9], (N_A, D_TOK), 0.02),
        'w_spatial': nrm(ks[10], (N_A, A_GROUPS, CHUNK, CHUNK), CHUNK ** -0.5),
        'b_spatial': 1.0 + nrm(ks[11], (N_A, A_GROUPS, CHUNK), 0.1),
        'w_in_b': nrm(ks[12], (N_B, d, D_TOK + D_MEM), d ** -0.5),
        'w_kv': nrm(ks[13], (d, 2 * KV_WIDTH), d ** -0.5) * kv_scale,
        'sinks': nrm(ks[14], (N_B, N_Q_HEADS), 0.5),
        'w_mem_kv': nrm(ks[15], (DEPTH, d, 2 * D_MEM), d ** -0.5) * mem_scale,
        'w_out': nrm(ks[16], (DEPTH, d, d), d ** -0.5 * BETA),
        'ln_g': 1.0 + nrm(ks[17], (DEPTH, 2, d), 0.05),
        'ln_b': nrm(ks[18], (DEPTH, 2, d), 0.02),
        'router_w': nrm(ks[19], (DEPTH, d, N_EXPERTS), d ** -0.5),
        'router_b': nrm(ks[20], (DEPTH, N_EXPERTS), 0.01),
        'w_gate_up': nrm(ks[21], (DEPTH, N_EXPERTS, d, 2 * D_EXPERT), d ** -0.5),
        'w_down': nrm(ks[22], (DEPTH, N_EXPERTS, D_EXPERT, d), D_EXPERT ** -0.5 * BETA),
        'shared_gate_up': nrm(ks[23], (DEPTH, d, 2 * D_SHARED), d ** -0.5),
        'shared_down': nrm(ks[24], (DEPTH, D_SHARED, d), D_SHARED ** -0.5 * BETA),
    }


def reference(x_prompt, x_sample, mem_prompt, cache_mem_k, cache_mem_v, cache_win_k, cache_win_v,
              w_in_a, sgu_g, sgu_b, w_spatial, b_spatial, w_in_b, w_kv, sinks, w_mem_kv, w_out,
              ln_g, ln_b, router_w, router_b, w_gate_up, w_down, shared_gate_up, shared_down):
    mkv = jnp.einsum('nmd,lde->lnme', mem_prompt, w_mem_kv)
    mshape = (DEPTH, mem_prompt.shape[0], MEM_LEN, MEM_HEADS, MEM_HEAD_DIM)
    state_mem_k = mkv[..., :D_MEM].reshape(mshape)
    state_mem_v = mkv[..., D_MEM:].reshape(mshape)
    weights = (w_in_a, sgu_g, sgu_b, w_spatial, b_spatial, w_in_b, w_kv, sinks, w_out, ln_g, ln_b,
               router_w, router_b, w_gate_up, w_down, shared_gate_up, shared_down)
    y_prompt, _, state_win_k_prompt, state_win_v_prompt = trunk(
        x_prompt, state_mem_k, state_mem_v, None, None, *weights)
    y_sample, chunk_v_sample, state_win_k_sample, state_win_v_sample = trunk(
        x_sample, cache_mem_k, cache_mem_v, cache_win_k, cache_win_v, *weights)
    state_chunk_v_sample = jnp.stack(chunk_v_sample)
    return (y_prompt, y_sample, state_mem_k, state_mem_v, state_win_k_prompt, state_win_v_prompt,
            state_win_k_sample, state_win_v_sample, state_chunk_v_sample)
```

```python
import functools

import numpy as np
import jax
import jax.numpy as jnp
from jax import lax
from jax.experimental import pallas as pl
from jax.experimental.pallas import tpu as pltpu

F32, BF16, I32, U32 = jnp.float32, jnp.bfloat16, jnp.int32, jnp.uint32

LANES = 128
TOP_K = 8
N_GROUPS = 8
TOPK_GROUPS = 4
ROUTED_SCALE = 2.5
WINDOW = 128
LN_EPS = 1e-5
MOE_BM = 256
VMEM_LIMIT = 56 * 1024 * 1024


def _pick(n, prefs):
    for p in prefs:
        if n % p == 0:
            return p
    return n


def _cparams(n_axes=1):
    return pltpu.CompilerParams(dimension_semantics=("arbitrary",) * n_axes,
                                vmem_limit_bytes=VMEM_LIMIT)


def _layer_norm(xf, g, b):
    mu = jnp.mean(xf, axis=-1, keepdims=True)
    var = jnp.mean(jnp.square(xf - mu), axis=-1, keepdims=True)
    return (xf - mu) * lax.rsqrt(var + LN_EPS) * g + b


def _pack_rows(a):
    half = a.shape[1] // 2
    lo = pltpu.bitcast(a[:, :half].astype(BF16).astype(F32), U32)
    hi = pltpu.bitcast(a[:, half:].astype(BF16).astype(F32), U32)
    return (lo >> 16) | (hi & jnp.uint32(0xFFFF0000))


def _store_slabs(o_ref, u):
    t = u.shape[0]
    s = u.shape[1] // LANES
    for c in range(s):
        o_ref[pl.ds(c, t, stride=s), :] = u[:, c * LANES:(c + 1) * LANES]


def _unpack_words(u):
    lo = pltpu.bitcast(u << 16, F32)
    hi = pltpu.bitcast(u & jnp.uint32(0xFFFF0000), F32)
    return lo, hi


def _cast_block(src_ref, dst_ref):
    k = dst_ref.shape[0]
    ck = _pick(k, (512, 256, 128))

    def body(c, carry):
        rows = pl.ds(pl.multiple_of(c * ck, ck), ck)
        dst_ref[rows, :] = src_ref[0, rows, :].astype(BF16)
        return carry

    lax.fori_loop(0, k // ck, body, 0)


def _gmm_body(e_ref, j_ref, r_ref, eo_ref, first_ref, valid_ref, x_ref, *rest, gated, act, slabs):
    del e_ref, j_ref, r_ref, eo_ref
    if gated:
        w_ref, w2_ref, o_ref, wb_ref, wb2_ref = rest
    else:
        w_ref, o_ref, wb_ref = rest
    i = pl.program_id(0)

    @pl.when(first_ref[i] == 1)
    def _cast():
        _cast_block(w_ref, wb_ref)
        if gated:
            _cast_block(w2_ref, wb2_ref)

    @pl.when(valid_ref[i] == 1)
    def _compute():
        x = x_ref[...]
        a = jnp.dot(x, wb_ref[...], preferred_element_type=F32)
        if gated:
            b = jnp.dot(x, wb2_ref[...], preferred_element_type=F32)
            a = jax.nn.silu(a) * b
        elif act == "gelu":
            a = jax.nn.gelu(a)
        if slabs:
            _store_slabs(o_ref.at[0], _pack_rows(a))
        else:
            o_ref[0] = a.astype(o_ref.dtype)


def _gmm(tables, x, w, *, tm, tn, n_cols, col0=0, col0_up=None, act=None, out_dtype=F32, n_out=1,
         e0=0, slabs=False):
    m, k = x.shape
    gated = col0_up is not None
    n_items = tables[0].shape[0]
    assert m % tm == 0 and n_cols % tn == 0 and col0 % tn == 0
    if slabs:
        assert tn == n_cols and n_cols % (2 * LANES) == 0
        spr = n_cols // (2 * LANES)
        out_spec = pl.BlockSpec((1, tm * spr, LANES), lambda i, e, j, r, eo, f, v: (eo[i], r[i], 0))
        out_shape = jax.ShapeDtypeStruct((n_out, m * spr, LANES), U32)
    else:
        out_spec = pl.BlockSpec((1, tm, tn), lambda i, e, j, r, eo, f, v: (eo[i], r[i], j[i]))
        out_shape = jax.ShapeDtypeStruct((n_out, m, n_cols), out_dtype)
    jb0 = col0 // tn
    in_specs = [
        pl.BlockSpec((tm, k), lambda i, e, j, r, eo, f, v: (r[i], 0)),
        pl.BlockSpec((1, k, tn), lambda i, e, j, r, eo, f, v: (e0 + e[i], 0, jb0 + j[i])),
    ]
    operands = [x, w]
    scratch = [pltpu.VMEM((k, tn), BF16)]
    if gated:
        assert col0_up % tn == 0
        jb1 = col0_up // tn
        in_specs.append(pl.BlockSpec((1, k, tn), lambda i, e, j, r, eo, f, v: (e0 + e[i], 0, jb1 + j[i])))
        operands.append(w)
        scratch.append(pltpu.VMEM((k, tn), BF16))
    grid_spec = pltpu.PrefetchScalarGridSpec(
        num_scalar_prefetch=6, grid=(n_items,), in_specs=in_specs, out_specs=out_spec,
        scratch_shapes=scratch)
    return pl.pallas_call(
        functools.partial(_gmm_body, gated=gated, act=act, slabs=slabs),
        grid_spec=grid_spec, out_shape=out_shape, compiler_params=_cparams(),
    )(*tables, *operands)


def _dense_tables(n_j, n_r, e=0):
    j = np.repeat(np.arange(n_j), n_r)
    r = np.tile(np.arange(n_r), n_j)
    z = np.zeros_like(j)
    cols = (z + e, j, r, z, (r == 0).astype(np.int64), z + 1)
    return tuple(jnp.asarray(c, I32) for c in cols)


def _dense(x, w, e, *, n_cols, col0=0, col0_up=None, act=None, out_dtype=F32, tm=None, tn=None):
    m = x.shape[0]
    tm = tm or _pick(m, (768, 512, 256, 128))
    tn = tn or _pick(n_cols, (512, 384, 256, 128))
    tables = _dense_tables(n_cols // tn, m // tm, e)
    return _gmm(tables, x, w, tm=tm, tn=tn, n_cols=n_cols, col0=col0, col0_up=col0_up,
                act=act, out_dtype=out_dtype)[0]


def _resid_ln_body(x_ref, y_ref, g_ref, b_ref, o_ref, ob_ref, os_ref, *, alpha):
    o = _layer_norm(alpha * x_ref[...] + y_ref[...], g_ref[...], b_ref[...])
    o_ref[...] = o
    ob_ref[...] = o.astype(BF16)
    _store_slabs(os_ref, _pack_rows(o))


def _resid_ln(x, y, g, b, alpha):
    m, d = x.shape
    tm = _pick(m, (256, 128))
    spr = d // (2 * LANES)
    row = pl.BlockSpec((tm, d), lambda i: (i, 0))
    vec = pl.BlockSpec((1, d), lambda i: (0, 0))
    return pl.pallas_call(
        functools.partial(_resid_ln_body, alpha=alpha),
        grid=(m // tm,), in_specs=[row, row, vec, vec],
        out_specs=[row, row, pl.BlockSpec((tm * spr, LANES), lambda i: (i, 0))],
        out_shape=[jax.ShapeDtypeStruct((m, d), F32), jax.ShapeDtypeStruct((m, d), BF16),
                   jax.ShapeDtypeStruct((m * spr, LANES), U32)],
        compiler_params=_cparams(),
    )(x, y, g.reshape(1, d), b.reshape(1, d))


def _router_body(x_ref, rw_ref, rb_ref, eid_ref, pos_ref, gate_ref, cnt_ref, base_ref, *, n_exp):
    tm = x_ref.shape[0]
    gsz = n_exp // N_GROUPS
    i = pl.program_id(0)

    @pl.when(i == 0)
    def _init():
        base_ref[...] = jnp.zeros_like(base_ref)

    logits = jnp.dot(x_ref[...], rw_ref[...], precision=lax.Precision.HIGHEST,
                     preferred_element_type=F32)
    s = jax.nn.sigmoid(logits)
    sb = s + rb_ref[...]
    lane = lax.broadcasted_iota(I32, (tm, n_exp), 1)
    assert gsz & (gsz - 1) == 0
    grp = lax.shift_right_logical(lane, gsz.bit_length() - 1)

    rank_g = jnp.zeros((tm, n_exp), I32)
    for e2 in range(n_exp):
        c = sb[:, e2:e2 + 1]
        beats = (c > sb) | ((c == sb) & (e2 < lane))
        rank_g = rank_g + jnp.where(beats & (grp == e2 // gsz), 1, 0)
    top2 = jnp.where(rank_g < 2, sb, 0.0)
    gsum = jnp.zeros((tm, n_exp), F32)
    for e2 in range(n_exp):
        gsum = gsum + jnp.where(grp == e2 // gsz, top2[:, e2:e2 + 1], 0.0)
    n_beat = jnp.zeros((tm, n_exp), I32)
    for g2 in range(N_GROUPS):
        c = gsum[:, g2 * gsz:g2 * gsz + 1]
        beats = (c > gsum) | ((c == gsum) & (g2 < grp))
        n_beat = n_beat + jnp.where(beats, 1, 0)
    v = jnp.where(n_beat < TOPK_GROUPS, sb, -jnp.inf)
    rank = jnp.zeros((tm, n_exp), I32)
    for e2 in range(n_exp):
        c = v[:, e2:e2 + 1]
        beats = (c > v) | ((c == v) & (e2 < lane))
        rank = rank + jnp.where(beats, 1, 0)
    sel = rank < TOP_K
    w = jnp.where(sel, s, 0.0)
    gate = w / jnp.sum(w, axis=-1, keepdims=True) * ROUTED_SCALE

    sel_b = jnp.where(sel, 1.0, 0.0).astype(BF16)
    ri = lax.broadcasted_iota(I32, (tm, tm), 0)
    ci = lax.broadcasted_iota(I32, (tm, tm), 1)
    lower = jnp.where(ri > ci, 1.0, 0.0).astype(BF16)
    local = jnp.dot(lower, sel_b, preferred_element_type=F32)
    pos = base_ref[...] + local
    base_ref[...] = base_ref[...] + jnp.sum(sel_b.astype(F32), axis=0, keepdims=True)
    cnt_ref[...] = base_ref[...].astype(I32)

    ui = lax.broadcasted_iota(I32, (n_exp, n_exp), 0)
    uj = lax.broadcasted_iota(I32, (n_exp, n_exp), 1)
    upper = jnp.where(ui <= uj, 1.0, 0.0).astype(BF16)
    slot = jnp.dot(sel_b, upper, preferred_element_type=F32)
    lane_f = lane.astype(F32)
    for k in range(TOP_K):
        m = sel & (slot == float(k + 1))
        eid_ref[:, k:k + 1] = jnp.sum(jnp.where(m, lane_f, 0.0), axis=-1, keepdims=True).astype(I32)
        pos_ref[:, k:k + 1] = jnp.sum(jnp.where(m, pos, 0.0), axis=-1, keepdims=True).astype(I32)
        gate_ref[:, k:k + 1] = jnp.sum(jnp.where(m, gate, 0.0), axis=-1, keepdims=True)


def _router(x, rw, rb):
    m, d = x.shape
    n_exp = rw.shape[1]
    tm = _pick(m, (256, 128))
    slot = pl.BlockSpec((tm, TOP_K), lambda i: (i, 0))
    return pl.pallas_call(
        functools.partial(_router_body, n_exp=n_exp),
        grid=(m // tm,),
        in_specs=[pl.BlockSpec((tm, d), lambda i: (i, 0)),
                  pl.BlockSpec((d, n_exp), lambda i: (0, 0)),
                  pl.BlockSpec((1, n_exp), lambda i: (0, 0))],
        out_specs=[slot, slot, slot, pl.BlockSpec((1, n_exp), lambda i: (0, 0))],
        out_shape=[jax.ShapeDtypeStruct((m, TOP_K), I32), jax.ShapeDtypeStruct((m, TOP_K), I32),
                   jax.ShapeDtypeStruct((m, TOP_K), F32), jax.ShapeDtypeStruct((1, n_exp), I32)],
        scratch_shapes=[pltpu.VMEM((1, n_exp), F32)],
        compiler_params=_cparams(),
    )(x, rw, rb.reshape(1, n_exp))


def _gather_body(nblk_ref, tok_ref, x_hbm, o_ref, buf, sem):
    i = pl.program_id(0)
    bm, spr = buf.shape[0], buf.shape[1]

    def row_copy(r):
        return pltpu.make_async_copy(x_hbm.at[tok_ref[0, 0, r]], buf.at[r], sem)

    @pl.when(i < nblk_ref[0])
    def _block():
        def start(r, c):
            row_copy(r).start()
            return c

        def wait(r, c):
            row_copy(r).wait()
            return c

        lax.fori_loop(0, bm, start, 0, unroll=8)
        lax.fori_loop(0, bm, wait, 0, unroll=8)
        los, his = [], []
        for c in range(spr):
            lo, hi = _unpack_words(buf[:, c, :])
            los.append(lo.astype(BF16))
            his.append(hi.astype(BF16))
        o_ref[...] = jnp.concatenate(los + his, axis=1)


def _gather_rows(x_slabs, row_tok, n_blocks, bm):
    nb_max = row_tok.shape[0] // bm
    _, spr, _ = x_slabs.shape
    d = spr * 2 * LANES
    blk = lambda i, nb: (jnp.minimum(i, nb[0] - 1), 0, 0)
    grid_spec = pltpu.PrefetchScalarGridSpec(
        num_scalar_prefetch=1, grid=(nb_max,),
        in_specs=[pl.BlockSpec((1, 1, bm), blk, memory_space=pltpu.SMEM),
                  pl.BlockSpec(memory_space=pl.ANY)],
        out_specs=pl.BlockSpec((bm, d), lambda i, nb: (jnp.minimum(i, nb[0] - 1), 0)),
        scratch_shapes=[pltpu.VMEM((bm, spr, LANES), U32), pltpu.SemaphoreType.DMA(())])
    return pl.pallas_call(
        _gather_body, grid_spec=grid_spec,
        out_shape=jax.ShapeDtypeStruct((nb_max * bm, d), BF16),
        compiler_params=_cparams(),
    )(n_blocks.reshape(1), row_tok.reshape(nb_max, 1, bm), x_slabs)


def _combine_body(dest_ref, gate_ref, x_ref, sh_ref, g_ref, b_ref, y_hbm, o_ref, ob_ref, buf, sem, *, alpha):
    tt, spr = buf.shape[1], buf.shape[2]

    def row_copy(t, k):
        return pltpu.make_async_copy(y_hbm.at[dest_ref[0, 0, t * TOP_K + k]], buf.at[k, t], sem)

    def start(t, c):
        for k in range(TOP_K):
            row_copy(t, k).start()
        return c

    def wait(t, c):
        for k in range(TOP_K):
            row_copy(t, k).wait()
        return c

    lax.fori_loop(0, tt, start, 0)
    acc = alpha * x_ref[...] + sh_ref[...]
    lax.fori_loop(0, tt, wait, 0)
    los, his = [None] * spr, [None] * spr
    for k in range(TOP_K):
        gk = gate_ref[:, k:k + 1]
        for c in range(spr):
            lo, hi = _unpack_words(buf[k, :, c, :])
            los[c] = gk * lo if k == 0 else los[c] + gk * lo
            his[c] = gk * hi if k == 0 else his[c] + gk * hi
    routed = jnp.concatenate(los + his, axis=1)
    o = _layer_norm(acc + routed, g_ref[...], b_ref[...])
    o_ref[...] = o
    ob_ref[...] = o.astype(BF16)


def _combine_ln(dest, gate, x, shared, y_slabs, g, b, alpha):
    m, d = x.shape
    spr = y_slabs.shape[1]
    tt = _pick(m, (64, 32, 16, 8))
    row = pl.BlockSpec((tt, d), lambda i: (i, 0))
    vec = pl.BlockSpec((1, d), lambda i: (0, 0))
    return pl.pallas_call(
        functools.partial(_combine_body, alpha=alpha),
        grid=(m // tt,),
        in_specs=[pl.BlockSpec((1, 1, tt * TOP_K), lambda i: (i, 0, 0), memory_space=pltpu.SMEM),
                  pl.BlockSpec((tt, TOP_K), lambda i: (i, 0)), row, row, vec, vec,
                  pl.BlockSpec(memory_space=pl.ANY)],
        out_specs=[row, row],
        out_shape=[jax.ShapeDtypeStruct((m, d), F32), jax.ShapeDtypeStruct((m, d), BF16)],
        scratch_shapes=[pltpu.VMEM((TOP_K, tt, spr, LANES), U32), pltpu.SemaphoreType.DMA(())],
        compiler_params=_cparams(),
    )(dest.reshape(m // tt, 1, tt * TOP_K), gate, x, shared, g.reshape(1, d), b.reshape(1, d), y_slabs)


def _moe_tables(counts, n_j, nb_max, bm):
    n_exp = counts.shape[0]
    nb = (counts + bm - 1) // bm
    blk_end = jnp.cumsum(nb)
    blk_start = blk_end - nb
    total = blk_end[-1]
    i = jnp.arange(n_j * nb_max, dtype=I32)
    valid = i < n_j * total
    e = jnp.minimum(jnp.searchsorted(n_j * blk_end, i, side="right"), n_exp - 1).astype(I32)
    local = i - n_j * blk_start[e]
    nbe = jnp.maximum(nb[e], 1)
    j = local // nbe
    rb = local % nbe
    r = blk_start[e] + rb
    last = jnp.maximum(n_j * total - 1, 0)
    fix = lambda a: jnp.where(valid, a, a[last]).astype(I32)
    first = ((rb == 0) & valid).astype(I32)
    return (fix(e), fix(j), fix(r), jnp.zeros_like(i), first, valid.astype(I32)), blk_start, total


def _moe_ffn(xf, xb, x_slabs, l, p, ln_g, ln_b, alpha):
    m, d = xf.shape
    spr = d // (2 * LANES)
    n_exp = p["router_w"].shape[-1]
    f = p["w_down"].shape[2]
    bm = MOE_BM
    nb_max = (m * TOP_K) // bm + n_exp
    eid, pos, gate, counts = _router(xf, p["router_w"][l], p["router_b"][l])
    cn = _pick(f, (384, 256, 128))
    up_tables, blk_start, total = _moe_tables(counts[0], f // cn, nb_max, bm)
    dn_tables, _, _ = _moe_tables(counts[0], 1, nb_max, bm)
    dest = bm * blk_start[eid] + pos
    tok = jnp.broadcast_to(jnp.arange(m, dtype=I32)[:, None], (m, TOP_K))
    row_tok = jnp.zeros((nb_max * bm,), I32).at[dest.reshape(-1)].set(
        tok.reshape(-1), unique_indices=True)
    xs = _gather_rows(x_slabs.reshape(m, spr, LANES), row_tok, total.astype(I32), bm)
    w_gu = p["w_gate_up"].reshape(-1, d, 2 * f)
    w_dn = p["w_down"].reshape(-1, f, d)
    h = _gmm(up_tables, xs, w_gu, tm=bm, tn=cn, n_cols=f, col0=0, col0_up=f,
             out_dtype=BF16, e0=l * n_exp)[0]
    y = _gmm(dn_tables, h, w_dn, tm=bm, tn=d, n_cols=d, e0=l * n_exp, slabs=True)[0]
    y = y.reshape(nb_max * bm, spr, LANES)
    hs = _dense(xb, p["shared_gate_up"], l, n_cols=f, col0=0, col0_up=f, out_dtype=BF16, tn=cn)
    ys = _dense(hs, p["shared_down"], l, n_cols=d, out_dtype=F32, tn=_pick(d, (1024, 512, 256, 128)))
    return _combine_ln(dest, gate, xf, ys, y, ln_g, ln_b, alpha)


def _sgu_body(hu_ref, hv_ref, g_ref, b_ref, wm_ref, bias_ref, o_ref, v_ref, *, n_groups):
    v = _layer_norm(hv_ref[...], g_ref[...], b_ref[...])
    v_ref[...] = v
    vb = v.astype(BF16)
    gd = v.shape[1] // n_groups
    for g in range(n_groups):
        cols = slice(g * gd, (g + 1) * gd)
        mixed = jnp.dot(wm_ref[0, g].astype(BF16), vb[:, cols], preferred_element_type=F32)
        mixed = mixed + bias_ref[0][:, g:g + 1]
        o_ref[:, cols] = (hu_ref[:, cols].astype(F32) * mixed).astype(BF16)


def _sgu(hu, hv, g, b, wmix, bias, n_prompt_blocks):
    m, dt = hv.shape
    c = wmix.shape[-1]
    ng = wmix.shape[1]
    nblk = m // c
    npb = n_prompt_blocks
    row = pl.BlockSpec((c, dt), lambda i: (i, 0))
    vec = pl.BlockSpec((1, dt), lambda i: (0, 0))
    kind = lambda i: jnp.where(i >= npb, 1, 0)
    return pl.pallas_call(
        functools.partial(_sgu_body, n_groups=ng),
        grid=(nblk,),
        in_specs=[row, row, vec, vec,
                  pl.BlockSpec((1, ng, c, c), lambda i: (kind(i), 0, 0, 0)),
                  pl.BlockSpec((1, c, ng), lambda i: (kind(i), 0, 0))],
        out_specs=[row, pl.BlockSpec((c, dt), lambda i: (jnp.maximum(i - npb, 0), 0))],
        out_shape=[jax.ShapeDtypeStruct((m, dt), BF16),
                   jax.ShapeDtypeStruct(((nblk - npb) * c, dt), F32)],
        compiler_params=_cparams(),
    )(hu, hv, g.reshape(1, dt), b.reshape(1, dt), wmix, bias)


def _softmax_rows(s):
    mx = jnp.max(s, axis=-1, keepdims=True)
    p = jnp.exp(s - mx)
    return p / jnp.sum(p, axis=-1, keepdims=True)


def _mem_heads(q, k_ref, v_ref, n_heads):
    dh = q.shape[1] // n_heads
    outs = []
    for hh in range(n_heads):
        cols = slice(hh * dh, (hh + 1) * dh)
        s = lax.dot_general(q[:, cols], k_ref[:, cols].astype(BF16), (((1,), (1,)), ((), ())),
                            preferred_element_type=F32) * dh ** -0.5
        p = _softmax_rows(s).astype(BF16)
        outs.append(jnp.dot(p, v_ref[:, cols].astype(BF16), preferred_element_type=F32))
    return jnp.concatenate(outs, axis=1)


def _mem_prompt_body(q_ref, k_ref, v_ref, o_ref, *, n_heads):
    o_ref[...] = _mem_heads(q_ref[...], k_ref.at[0, 0], v_ref.at[0, 0], n_heads).astype(BF16)


def _mem_attn_prompt(q, mem_k, mem_v, l, n_heads, n_batch, seq):
    dq = q.shape[1]
    mlen = mem_k.shape[2]
    tq = _pick(seq, (512, 256, 128))
    nt = seq // tq
    kv = pl.BlockSpec((1, 1, mlen, dq), lambda n, i: (l, n, 0, 0))
    return pl.pallas_call(
        functools.partial(_mem_prompt_body, n_heads=n_heads),
        grid=(n_batch, nt),
        in_specs=[pl.BlockSpec((tq, dq), lambda n, i: (n * nt + i, 0)), kv, kv],
        out_specs=pl.BlockSpec((tq, dq), lambda n, i: (n * nt + i, 0)),
        out_shape=jax.ShapeDtypeStruct((n_batch * seq, dq), BF16),
        compiler_params=_cparams(2),
    )(q, mem_k, mem_v)


def _mem_sample_body(q_ref, k_ref, v_ref, o_ref, *, n_heads, seq):
    q = q_ref[...]
    rows = lax.broadcasted_iota(I32, (q.shape[0], 1), 0)
    out = jnp.zeros(q.shape, F32)
    for s in range(q.shape[0] // seq):
        o = _mem_heads(q, k_ref.at[0, s], v_ref.at[0, s], n_heads)
        out = jnp.where((rows >= s * seq) & (rows < (s + 1) * seq), o, out)
    o_ref[...] = out.astype(BF16)


def _mem_attn_sample(q, mem_k, mem_v, l, n_heads, row0, seq):
    dq = q.shape[1]
    n_seq, mlen = mem_k.shape[1], mem_k.shape[2]
    per = max(16 // seq, 1)
    tq = per * seq
    assert row0 % tq == 0 and n_seq % per == 0
    kv = pl.BlockSpec((1, per, mlen, dq), lambda i: (l, i, 0, 0))
    return pl.pallas_call(
        functools.partial(_mem_sample_body, n_heads=n_heads, seq=seq),
        grid=(n_seq // per,),
        in_specs=[pl.BlockSpec((tq, dq), lambda i: (row0 // tq + i, 0)), kv, kv],
        out_specs=pl.BlockSpec((tq, dq), lambda i: (i, 0)),
        out_shape=jax.ShapeDtypeStruct((n_seq * seq, dq), BF16),
        compiler_params=_cparams(),
    )(q, mem_k, mem_v)


def _sink_attend(s, mask, sink, vb):
    s = jnp.where(mask, s, -jnp.inf)
    mx = jnp.maximum(jnp.max(s, axis=-1, keepdims=True), sink)
    p = jnp.exp(s - mx)
    den = jnp.sum(p, axis=-1, keepdims=True) + jnp.exp(sink - mx)
    return jnp.dot((p / den).astype(BF16), vb, preferred_element_type=F32)


def _win_prompt_body(sink_ref, q_ref, kvp_ref, kvc_ref, o_ref, *, n_kv, q_per_kv, dh, sink_row):
    b = pl.program_id(1)
    w = q_ref.shape[0]
    kvw = n_kv * dh
    qi = lax.broadcasted_iota(I32, (w, 2 * w), 0)
    kj = lax.broadcasted_iota(I32, (w, 2 * w), 1)
    mask = (kj > qi) & (kj <= qi + w) & ((kj >= w) | (b > 0))
    outs = []
    for h in range(n_kv):
        kc = slice(h * dh, (h + 1) * dh)
        vc = slice(kvw + h * dh, kvw + (h + 1) * dh)
        kb = jnp.concatenate([kvp_ref[:, kc], kvc_ref[:, kc]], axis=0).astype(BF16)
        vb = jnp.concatenate([kvp_ref[:, vc], kvc_ref[:, vc]], axis=0).astype(BF16)
        for g in range(q_per_kv):
            hq = h * q_per_kv + g
            s = lax.dot_general(q_ref[:, hq * dh:(hq + 1) * dh], kb, (((1,), (1,)), ((), ())),
                                preferred_element_type=F32) * dh ** -0.5
            outs.append(_sink_attend(s, mask, sink_ref[sink_row, hq], vb))
    o_ref[...] = jnp.concatenate(outs, axis=1).astype(BF16)


def _win_attn_prompt(q, kv, sinks, jl, n_batch, seq, n_kv, dh):
    dq = q.shape[1]
    q_per_kv = dq // (n_kv * dh)
    nb = seq // WINDOW
    kvw = kv.shape[1]
    grid_spec = pltpu.PrefetchScalarGridSpec(
        num_scalar_prefetch=1, grid=(n_batch, nb),
        in_specs=[pl.BlockSpec((WINDOW, dq), lambda n, b, s: (n * nb + b, 0)),
                  pl.BlockSpec((WINDOW, kvw), lambda n, b, s: (n * nb + jnp.maximum(b - 1, 0), 0)),
                  pl.BlockSpec((WINDOW, kvw), lambda n, b, s: (n * nb + b, 0))],
        out_specs=pl.BlockSpec((WINDOW, dq), lambda n, b, s: (n * nb + b, 0)))
    return pl.pallas_call(
        functools.partial(_win_prompt_body, n_kv=n_kv, q_per_kv=q_per_kv, dh=dh, sink_row=jl),
        grid_spec=grid_spec,
        out_shape=jax.ShapeDtypeStruct((n_batch * seq, dq), BF16),
        compiler_params=_cparams(2),
    )(sinks, q, kv, kv)


def _win_sample_body(q_ref, k_ref, v_ref, sink_ref, o_ref, *, n_kv, seq, n_buf):
    rows = q_ref.shape[2]
    nk = k_ref.shape[2]
    assert seq & (seq - 1) == 0
    t = lax.broadcasted_iota(I32, (rows, nk), 0) & (seq - 1)
    kj = lax.broadcasted_iota(I32, (rows, nk), 1)
    diff = t - (kj - n_buf)
    mask = (diff >= 0) & (diff < WINDOW)
    dh = q_ref.shape[3]
    for h in range(n_kv):
        s = lax.dot_general(q_ref[0, h].astype(BF16), k_ref[0, h].astype(BF16),
                            (((1,), (1,)), ((), ())), preferred_element_type=F32) * dh ** -0.5
        o_ref[0, h] = _sink_attend(s, mask, sink_ref[0, h], v_ref[0, h].astype(BF16))


def _win_attn_sample(q, kk, vv, sink_rows, seq, n_buf):
    n, n_kv, rows, dh = q.shape
    nk = kk.shape[2]
    blk = lambda r: pl.BlockSpec((1, n_kv, r, dh), lambda i: (i, 0, 0, 0))
    return pl.pallas_call(
        functools.partial(_win_sample_body, n_kv=n_kv, seq=seq, n_buf=n_buf),
        grid=(n,),
        in_specs=[blk(rows), blk(nk), blk(nk),
                  pl.BlockSpec((1, n_kv, rows, 1), lambda i: (0, 0, 0, 0))],
        out_specs=blk(rows),
        out_shape=jax.ShapeDtypeStruct((n, n_kv, rows, dh), F32),
        compiler_params=_cparams(),
    )(q, kk, vv, sink_rows)


def kernel(x_prompt, x_sample, mem_prompt, cache_mem_k, cache_mem_v, cache_win_k, cache_win_v, w_in_a, sgu_g, sgu_b, w_spatial, b_spatial, w_in_b, w_kv, sinks, w_mem_kv, w_out, ln_g, ln_b, router_w, router_b, w_gate_up, w_down, shared_gate_up, shared_down):
    nb, seq, d = x_prompt.shape
    ns, sseq, _ = x_sample.shape
    depth = w_out.shape[0]
    n_a = w_in_a.shape[0]
    mem_len = mem_prompt.shape[1]
    mem_heads, mem_dh = cache_mem_k.shape[3], cache_mem_k.shape[4]
    d_mem = mem_heads * mem_dh
    d_tok = d - d_mem
    n_buf, n_kv, dh = cache_win_k.shape[1], cache_win_k.shape[2], cache_win_k.shape[3]
    kvw = n_kv * dh
    q_per_kv = d_tok // kvw
    chunk = w_spatial.shape[2]
    n_groups = w_spatial.shape[1]
    alpha = float((2 * depth) ** 0.25)
    mp, ms = nb * seq, ns * sseq
    m = mp + ms
    p = dict(router_w=router_w, router_b=router_b, w_gate_up=w_gate_up, w_down=w_down,
             shared_gate_up=shared_gate_up, shared_down=shared_down)

    memb = mem_prompt.reshape(nb * mem_len, d).astype(BF16)
    tm_mem = _pick(nb * mem_len, (512, 256, 128))
    tn_mem = _pick(d_mem, (512, 256, 128))
    n_j, n_r = d_mem // tn_mem, nb * mem_len // tm_mem
    lay = np.repeat(np.arange(depth), n_j * n_r)
    jj = np.tile(np.repeat(np.arange(n_j), n_r), depth)
    rr = np.tile(np.arange(n_r), depth * n_j)
    mem_tables = tuple(jnp.asarray(c, I32) for c in
                       (lay, jj, rr, lay, (rr == 0).astype(np.int64), np.ones_like(lay)))
    mem_kw = dict(tm=tm_mem, tn=tn_mem, n_cols=d_mem, out_dtype=F32, n_out=depth)
    pmem_k = _gmm(mem_tables, memb, w_mem_kv, col0=0, **mem_kw).reshape(depth, nb, mem_len, d_mem)
    pmem_v = _gmm(mem_tables, memb, w_mem_kv, col0=d_mem, **mem_kw).reshape(depth, nb, mem_len, d_mem)
    smem_k = cache_mem_k.reshape(depth, ns, mem_len, d_mem)
    smem_v = cache_mem_v.reshape(depth, ns, mem_len, d_mem)

    xf = jnp.concatenate([x_prompt.reshape(mp, d), x_sample.reshape(ms, d)], axis=0)
    xb = xf.astype(BF16)

    tril = jnp.tril(jnp.ones((chunk, chunk), F32))
    c_s = min(sseq, chunk)
    reps = chunk // c_s
    eye = jnp.eye(reps, dtype=F32)
    w_p = w_spatial * tril
    w_c = (w_spatial[:, :, :c_s, :c_s] * tril[:c_s, :c_s])
    w_s = jnp.einsum("ab,lgts->lgatbs", eye, w_c).reshape(n_a, n_groups, chunk, chunk)
    wmix = jnp.stack([w_p, w_s], axis=1)
    bias_p = jnp.swapaxes(b_spatial, 1, 2)
    bias_s = jnp.tile(bias_p[:, :c_s], (1, reps, 1))
    bias = jnp.stack([bias_p, bias_s], axis=1)

    chunk_v = []
    kv = k_new = v_new = kk = vv = None
    for l in range(depth):
        if l < n_a:
            hu = _dense(xb, w_in_a, l, n_cols=d_tok, col0=0, act="gelu", out_dtype=BF16)
            hv = _dense(xb, w_in_a, l, n_cols=d_tok, col0=d_tok, act="gelu", out_dtype=F32)
            qm = _dense(xb, w_in_a, l, n_cols=d_mem, col0=2 * d_tok, out_dtype=BF16)
            tok, v_s = _sgu(hu, hv, sgu_g[l], sgu_b[l], wmix[l], bias[l], mp // chunk)
            chunk_v.append(v_s.reshape(ns, sseq, d_tok))
            tok_p, tok_s = tok[:mp], tok[mp:]
        else:
            jl = l - n_a
            if kv is None:
                kv = _dense(xb, w_kv.reshape(1, d, 2 * kvw), 0, n_cols=2 * kvw, out_dtype=F32,
                            tn=_pick(2 * kvw, (384, 256, 128)))
                k_new = kv[:, :kvw]
                v_new = kv[:, kvw:]
                kk = jnp.concatenate([cache_win_k, k_new[mp:].reshape(ns, sseq, n_kv, dh)], axis=1)
                vv = jnp.concatenate([cache_win_v, v_new[mp:].reshape(ns, sseq, n_kv, dh)], axis=1)
            q = _dense(xb, w_in_b, jl, n_cols=d_tok, col0=0, out_dtype=BF16)
            qm = _dense(xb, w_in_b, jl, n_cols=d_mem, col0=d_tok, out_dtype=BF16)
            tok_p = _win_attn_prompt(q, kv, sinks, jl, nb, seq, n_kv, dh)
            q_s = q[mp:].astype(F32).reshape(ns, sseq, n_kv, q_per_kv, dh)
            q_s = q_s.transpose(0, 2, 3, 1, 4).reshape(ns, n_kv, q_per_kv * sseq, dh)
            sink_rows = jnp.repeat(sinks[jl].reshape(1, n_kv, q_per_kv), sseq, axis=2)[..., None]
            o_s = _win_attn_sample(q_s, kk.transpose(0, 2, 1, 3), vv.transpose(0, 2, 1, 3),
                                   sink_rows, sseq, n_buf)
            tok_s = o_s.reshape(ns, n_kv, q_per_kv, sseq, dh).transpose(0, 3, 1, 2, 4)
            tok_s = tok_s.reshape(ms, d_tok).astype(BF16)
        mo_p = _mem_attn_prompt(qm, pmem_k, pmem_v, l, mem_heads, nb, seq)
        mo_s = _mem_attn_sample(qm, smem_k, smem_v, l, mem_heads, mp, sseq)
        cat = jnp.concatenate([jnp.concatenate([tok_p, mo_p], axis=1),
                               jnp.concatenate([tok_s, mo_s], axis=1)], axis=0)
        mix = _dense(cat, w_out, l, n_cols=d, out_dtype=F32)
        xf, xb, x_slabs = _resid_ln(xf, mix, ln_g[l, 0], ln_b[l, 0], alpha)
        xf, xb = _moe_ffn(xf, xb, x_slabs, l, p, ln_g[l, 1], ln_b[l, 1], alpha)

    y_prompt = xf[:mp].reshape(nb, seq, d)
    y_sample = xf[mp:].reshape(ns, sseq, d)
    mshape = (depth, nb, mem_len, mem_heads, mem_dh)
    keep_p = min(WINDOW, seq)
    k_p = k_new[:mp].reshape(nb, seq, n_kv, dh)[:, -keep_p:]
    v_p = v_new[:mp].reshape(nb, seq, n_kv, dh)[:, -keep_p:]
    return (y_prompt, y_sample, pmem_k.reshape(mshape), pmem_v.reshape(mshape), k_p, v_p,
            kk[:, -n_buf:], vv[:, -n_buf:], jnp.stack(chunk_v))
```

```python
import functools

import numpy as np
import jax
import jax.numpy as jnp
from jax import lax
from jax.experimental import pallas as pl
from jax.experimental.pallas import tpu as pltpu

F32, BF16, I32, U32 = jnp.float32, jnp.bfloat16, jnp.int32, jnp.uint32

LANES = 128
TOP_K = 8
N_GROUPS = 8
TOPK_GROUPS = 4
ROUTED_SCALE = 2.5
WINDOW = 128
LN_EPS = 1e-5
MOE_BM = 256
VMEM_LIMIT = 56 * 1024 * 1024


def _pick(n, prefs):
    for p in prefs:
        if n % p == 0:
            return p
    return n


def _cparams(n_axes=1):
    return pltpu.CompilerParams(dimension_semantics=("arbitrary",) * n_axes,
                                vmem_limit_bytes=VMEM_LIMIT)


def _layer_norm(xf, g, b):
    mu = jnp.mean(xf, axis=-1, keepdims=True)
    var = jnp.mean(jnp.square(xf - mu), axis=-1, keepdims=True)
    return (xf - mu) * lax.rsqrt(var + LN_EPS) * g + b


def _pack_rows(a):
    half = a.shape[1] // 2
    lo = pltpu.bitcast(a[:, :half].astype(BF16).astype(F32), U32)
    hi = pltpu.bitcast(a[:, half:].astype(BF16).astype(F32), U32)
    return (lo >> 16) | (hi & jnp.uint32(0xFFFF0000))


def _store_slabs(o_ref, u):
    t = u.shape[0]
    s = u.shape[1] // LANES
    for c in range(s):
        o_ref[pl.ds(c, t, stride=s), :] = u[:, c * LANES:(c + 1) * LANES]


def _unpack_words(u):
    lo = pltpu.bitcast(u << 16, F32)
    hi = pltpu.bitcast(u & jnp.uint32(0xFFFF0000), F32)
    return lo, hi


def _cast_block(src_ref, dst_ref, col0):
    k, tn = src_ref.shape[1], src_ref.shape[2]
    ck = _pick(k, (512, 256, 128))

    def body(c, carry):
        rows = pl.ds(pl.multiple_of(c * ck, ck), ck)
        dst_ref[rows, col0:col0 + tn] = src_ref[0, rows, :].astype(BF16)
        return carry

    lax.fori_loop(0, k // ck, body, 0)


def _gmm_body(e_ref, j_ref, r_ref, eo_ref, first_ref, valid_ref, x_ref, *rest, gated, act, slabs):
    del e_ref, j_ref, r_ref, eo_ref
    if gated:
        w_ref, w2_ref, o_ref, wb_ref = rest
    else:
        w_ref, o_ref, wb_ref = rest
    tn = w_ref.shape[2]
    i = pl.program_id(0)

    @pl.when(first_ref[i] == 1)
    def _cast():
        _cast_block(w_ref, wb_ref, 0)
        if gated:
            _cast_block(w2_ref, wb_ref, tn)

    @pl.when(valid_ref[i] == 1)
    def _compute():
        a = jnp.dot(x_ref[...], wb_ref[...], preferred_element_type=F32)
        if gated:
            a = jax.nn.silu(a[:, :tn]) * a[:, tn:]
        elif act == "gelu":
            a = jax.nn.gelu(a)
        if slabs:
            _store_slabs(o_ref.at[0], _pack_rows(a))
        else:
            o_ref[0] = a.astype(o_ref.dtype)


def _gmm(tables, x, w, *, name, tm, tn, n_cols, col0=0, col0_up=None, act=None, out_dtype=F32,
         n_out=1, e0=0, slabs=False):
    m, k = x.shape
    gated = col0_up is not None
    n_items = tables[0].shape[0]
    assert m % tm == 0 and n_cols % tn == 0 and col0 % tn == 0
    if slabs:
        assert tn == n_cols and n_cols % (2 * LANES) == 0
        spr = n_cols // (2 * LANES)
        out_spec = pl.BlockSpec((1, tm * spr, LANES), lambda i, e, j, r, eo, f, v: (eo[i], r[i], 0))
        out_shape = jax.ShapeDtypeStruct((n_out, m * spr, LANES), U32)
    else:
        out_spec = pl.BlockSpec((1, tm, tn), lambda i, e, j, r, eo, f, v: (eo[i], r[i], j[i]))
        out_shape = jax.ShapeDtypeStruct((n_out, m, n_cols), out_dtype)
    jb0 = col0 // tn
    in_specs = [
        pl.BlockSpec((tm, k), lambda i, e, j, r, eo, f, v: (r[i], 0)),
        pl.BlockSpec((1, k, tn), lambda i, e, j, r, eo, f, v: (e0 + e[i], 0, jb0 + j[i])),
    ]
    operands = [x, w]
    if gated:
        assert col0_up % tn == 0 and tn % LANES == 0
        jb1 = col0_up // tn
        in_specs.append(pl.BlockSpec((1, k, tn), lambda i, e, j, r, eo, f, v: (e0 + e[i], 0, jb1 + j[i])))
        operands.append(w)
    scratch = [pltpu.VMEM((k, 2 * tn if gated else tn), BF16)]
    grid_spec = pltpu.PrefetchScalarGridSpec(
        num_scalar_prefetch=6, grid=(n_items,), in_specs=in_specs, out_specs=out_spec,
        scratch_shapes=scratch)
    return pl.pallas_call(
        functools.partial(_gmm_body, gated=gated, act=act, slabs=slabs),
        grid_spec=grid_spec, out_shape=out_shape, compiler_params=_cparams(), name=name,
    )(*tables, *operands)


def _dense_tables(n_j, n_r, e=0):
    j = np.repeat(np.arange(n_j), n_r)
    r = np.tile(np.arange(n_r), n_j)
    z = np.zeros_like(j)
    cols = (z + e, j, r, z, (r == 0).astype(np.int64), z + 1)
    return tuple(jnp.asarray(c, I32) for c in cols)


def _dense(x, w, e, *, name, n_cols, col0=0, col0_up=None, act=None, out_dtype=F32, tm=None, tn=None):
    m = x.shape[0]
    tm = tm or _pick(m, (768, 512, 256, 128))
    tn = tn or _pick(n_cols, (512, 384, 256, 128))
    tables = _dense_tables(n_cols // tn, m // tm, e)
    return _gmm(tables, x, w, name=name, tm=tm, tn=tn, n_cols=n_cols, col0=col0, col0_up=col0_up,
                act=act, out_dtype=out_dtype)[0]


def _resid_ln_body(x_ref, y_ref, g_ref, b_ref, o_ref, ob_ref, os_ref, *, alpha):
    o = _layer_norm(alpha * x_ref[...] + y_ref[...], g_ref[...], b_ref[...])
    o_ref[...] = o
    ob_ref[...] = o.astype(BF16)
    _store_slabs(os_ref, _pack_rows(o))


def _resid_ln(x, y, g, b, alpha):
    m, d = x.shape
    tm = _pick(m, (256, 128))
    spr = d // (2 * LANES)
    row = pl.BlockSpec((tm, d), lambda i: (i, 0))
    vec = pl.BlockSpec((1, d), lambda i: (0, 0))
    return pl.pallas_call(
        functools.partial(_resid_ln_body, alpha=alpha),
        grid=(m // tm,), in_specs=[row, row, vec, vec],
        out_specs=[row, row, pl.BlockSpec((tm * spr, LANES), lambda i: (i, 0))],
        out_shape=[jax.ShapeDtypeStruct((m, d), F32), jax.ShapeDtypeStruct((m, d), BF16),
                   jax.ShapeDtypeStruct((m * spr, LANES), U32)],
        compiler_params=_cparams(), name="resid_ln",
    )(x, y, g.reshape(1, d), b.reshape(1, d))


def _router_body(x_ref, rw_ref, rb_ref, eid_ref, pos_ref, gate_ref, cnt_ref, base_ref, *, n_exp):
    tm = x_ref.shape[0]
    gsz = n_exp // N_GROUPS
    i = pl.program_id(0)

    @pl.when(i == 0)
    def _init():
        base_ref[...] = jnp.zeros_like(base_ref)

    logits = jnp.dot(x_ref[...], rw_ref[...], precision=lax.Precision.HIGHEST,
                     preferred_element_type=F32)
    s = jax.nn.sigmoid(logits.T[:n_exp])
    sb = s + rb_ref[...]
    row = lax.broadcasted_iota(I32, (n_exp, tm), 0)
    assert gsz & (gsz - 1) == 0
    grp = lax.shift_right_logical(row, gsz.bit_length() - 1)
    bcast = lambda a, e2: jnp.broadcast_to(a[e2:e2 + 1, :], (n_exp, tm))

    def count_beaten(a, idx, same=None):
        n = jnp.zeros((n_exp, tm), I32)
        for e2 in range(n_exp):
            c = bcast(a, e2)
            beats = (c > a) | ((c == a) & (e2 < idx))
            if same is not None:
                beats = beats & (same == e2 // gsz)
            n = n + jnp.where(beats, 1, 0)
        return n

    top2 = jnp.where(count_beaten(sb, row, grp) < 2, sb, 0.0)
    gsum = jnp.zeros((n_exp, tm), F32)
    for e2 in range(n_exp):
        gsum = gsum + jnp.where(grp == e2 // gsz, bcast(top2, e2), 0.0)
    n_beat = jnp.zeros((n_exp, tm), I32)
    for g2 in range(N_GROUPS):
        c = bcast(gsum, g2 * gsz)
        n_beat = n_beat + jnp.where((c > gsum) | ((c == gsum) & (g2 < grp)), 1, 0)
    v = jnp.where(n_beat < TOPK_GROUPS, sb, -jnp.inf)
    sel = count_beaten(v, row) < TOP_K
    w = jnp.where(sel, s, 0.0)
    gate = w / jnp.sum(w, axis=0, keepdims=True) * ROUTED_SCALE

    sel_b = jnp.where(sel, 1.0, 0.0).astype(BF16)
    ti = lax.broadcasted_iota(I32, (tm, tm), 0)
    tj = lax.broadcasted_iota(I32, (tm, tm), 1)
    earlier = jnp.where(ti < tj, 1.0, 0.0).astype(BF16)
    pos = base_ref[:, 0:1] + jnp.dot(sel_b, earlier, preferred_element_type=F32)
    base_ref[...] = base_ref[...] + jnp.sum(sel_b.astype(F32), axis=1, keepdims=True)
    cnt_ref[...] = base_ref[...].astype(I32)

    ei = lax.broadcasted_iota(I32, (n_exp, n_exp), 0)
    ej = lax.broadcasted_iota(I32, (n_exp, n_exp), 1)
    upto = jnp.where(ej <= ei, 1.0, 0.0).astype(BF16)
    slot = jnp.dot(upto, sel_b, preferred_element_type=F32)
    row_f = row.astype(F32)
    for k in range(TOP_K):
        m = sel & (slot == float(k + 1))
        eid_ref[k:k + 1, :] = jnp.sum(jnp.where(m, row_f, 0.0), axis=0, keepdims=True).astype(I32)
        pos_ref[k:k + 1, :] = jnp.sum(jnp.where(m, pos, 0.0), axis=0, keepdims=True).astype(I32)
        gate_ref[k:k + 1, :] = jnp.sum(jnp.where(m, gate, 0.0), axis=0, keepdims=True)


def _router(x, rw, rb):
    m, d = x.shape
    n_exp = rw.shape[1]
    n_pad = -(-n_exp // LANES) * LANES
    tm = _pick(m, (256, 128))
    slot = pl.BlockSpec((TOP_K, tm), lambda i: (0, i))
    eid, pos, gate, cnt = pl.pallas_call(
        functools.partial(_router_body, n_exp=n_exp),
        grid=(m // tm,),
        in_specs=[pl.BlockSpec((tm, d), lambda i: (i, 0)),
                  pl.BlockSpec((d, n_pad), lambda i: (0, 0)),
                  pl.BlockSpec((n_exp, 1), lambda i: (0, 0))],
        out_specs=[slot, slot, slot, pl.BlockSpec((n_exp, LANES), lambda i: (0, 0))],
        out_shape=[jax.ShapeDtypeStruct((TOP_K, m), I32), jax.ShapeDtypeStruct((TOP_K, m), I32),
                   jax.ShapeDtypeStruct((TOP_K, m), F32), jax.ShapeDtypeStruct((n_exp, LANES), I32)],
        scratch_shapes=[pltpu.VMEM((n_exp, LANES), F32)],
        compiler_params=_cparams(), name="router",
    )(x, jnp.pad(rw, ((0, 0), (0, n_pad - n_exp))), rb.reshape(n_exp, 1))
    return eid.T, pos.T, gate.T, cnt[:, 0]


def _gather_body(nblk_ref, tok_ref, nxt_ref, x_hbm, o_ref, buf, sem):
    i = pl.program_id(0)
    nblk = nblk_ref[0]
    spr = x_hbm.shape[1]
    bm = buf.shape[0] // (2 * spr)
    slot = lax.rem(i, 2)

    def row_copy(ids_ref, s, r):
        dst = buf.at[pl.ds(pl.multiple_of((s * bm + r) * spr, spr), spr), :]
        return pltpu.make_async_copy(x_hbm.at[ids_ref[0, 0, r]], dst, sem.at[s])

    def start_all(ids_ref, s):
        def body(r, c):
            row_copy(ids_ref, s, r).start()
            return c

        lax.fori_loop(0, bm, body, 0, unroll=8)

    @pl.when(i == 0)
    def _first():
        start_all(tok_ref, 0)

    @pl.when(i + 1 < nblk)
    def _prefetch():
        start_all(nxt_ref, 1 - slot)

    @pl.when(i < nblk)
    def _block():
        def wait(r, c):
            row_copy(tok_ref, slot, r).wait()
            return c

        lax.fori_loop(0, bm, wait, 0, unroll=8)
        base = slot * (bm * spr)
        los, his = [], []
        for c in range(spr):
            lo, hi = _unpack_words(buf[pl.ds(base + c, bm, stride=spr), :])
            los.append(lo.astype(BF16))
            his.append(hi.astype(BF16))
        o_ref[...] = jnp.concatenate(los + his, axis=1)


def _gather_rows(x_slabs, row_tok, n_blocks, bm):
    nb_max = row_tok.shape[0] // bm
    _, spr, _ = x_slabs.shape
    d = spr * 2 * LANES
    ids = row_tok.reshape(nb_max, 1, bm)
    grid_spec = pltpu.PrefetchScalarGridSpec(
        num_scalar_prefetch=1, grid=(nb_max,),
        in_specs=[pl.BlockSpec((1, 1, bm), lambda i, nb: (jnp.minimum(i, nb[0] - 1), 0, 0),
                               memory_space=pltpu.SMEM),
                  pl.BlockSpec((1, 1, bm), lambda i, nb: (jnp.minimum(i + 1, nb[0] - 1), 0, 0),
                               memory_space=pltpu.SMEM),
                  pl.BlockSpec(memory_space=pl.ANY)],
        out_specs=pl.BlockSpec((bm, d), lambda i, nb: (jnp.minimum(i, nb[0] - 1), 0)),
        scratch_shapes=[pltpu.VMEM((2 * bm * spr, LANES), U32), pltpu.SemaphoreType.DMA((2,))])
    return pl.pallas_call(
        _gather_body, grid_spec=grid_spec,
        out_shape=jax.ShapeDtypeStruct((nb_max * bm, d), BF16),
        compiler_params=_cparams(), name="moe_gather",
    )(n_blocks.reshape(1), ids, ids, x_slabs)


def _combine_body(dest_ref, nxt_ref, gate_ref, x_ref, sh_ref, g_ref, b_ref, y_hbm, o_ref, ob_ref,
                  buf, sem, *, alpha):
    i = pl.program_id(0)
    tt = x_ref.shape[0]
    spr = y_hbm.shape[1]
    slot = lax.rem(i, 2)

    def row_copy(ids_ref, s, t, k):
        off = pl.multiple_of(((s * TOP_K + k) * tt + t) * spr, spr)
        return pltpu.make_async_copy(y_hbm.at[ids_ref[0, 0, t * TOP_K + k]],
                                     buf.at[pl.ds(off, spr), :], sem.at[s])

    def start_all(ids_ref, s):
        def body(t, c):
            for k in range(TOP_K):
                row_copy(ids_ref, s, t, k).start()
            return c

        lax.fori_loop(0, tt, body, 0)

    @pl.when(i == 0)
    def _first():
        start_all(dest_ref, 0)

    @pl.when(i + 1 < pl.num_programs(0))
    def _prefetch():
        start_all(nxt_ref, 1 - slot)

    acc = alpha * x_ref[...] + sh_ref[...]

    def wait(t, c):
        for k in range(TOP_K):
            row_copy(dest_ref, slot, t, k).wait()
        return c

    lax.fori_loop(0, tt, wait, 0)
    los, his = [None] * spr, [None] * spr
    for k in range(TOP_K):
        gk = gate_ref[:, k:k + 1]
        base = (slot * TOP_K + k) * (tt * spr)
        for c in range(spr):
            lo, hi = _unpack_words(buf[pl.ds(base + c, tt, stride=spr), :])
            los[c] = gk * lo if k == 0 else los[c] + gk * lo
            his[c] = gk * hi if k == 0 else his[c] + gk * hi
    routed = jnp.concatenate(los + his, axis=1)
    o = _layer_norm(acc + routed, g_ref[...], b_ref[...])
    o_ref[...] = o
    ob_ref[...] = o.astype(BF16)


def _combine_ln(dest, gate, x, shared, y_slabs, g, b, alpha):
    m, d = x.shape
    spr = y_slabs.shape[1]
    tt = _pick(m, (64, 32, 16, 8))
    n = m // tt
    row = pl.BlockSpec((tt, d), lambda i: (i, 0))
    vec = pl.BlockSpec((1, d), lambda i: (0, 0))
    ids = dest.reshape(n, 1, tt * TOP_K)
    return pl.pallas_call(
        functools.partial(_combine_body, alpha=alpha),
        grid=(n,),
        in_specs=[pl.BlockSpec((1, 1, tt * TOP_K), lambda i: (i, 0, 0), memory_space=pltpu.SMEM),
                  pl.BlockSpec((1, 1, tt * TOP_K), lambda i: (jnp.minimum(i + 1, n - 1), 0, 0),
                               memory_space=pltpu.SMEM),
                  pl.BlockSpec((tt, TOP_K), lambda i: (i, 0)), row, row, vec, vec,
                  pl.BlockSpec(memory_space=pl.ANY)],
        out_specs=[row, row],
        out_shape=[jax.ShapeDtypeStruct((m, d), F32), jax.ShapeDtypeStruct((m, d), BF16)],
        scratch_shapes=[pltpu.VMEM((2 * TOP_K * tt * spr, LANES), U32), pltpu.SemaphoreType.DMA((2,))],
        compiler_params=_cparams(), name="moe_combine",
    )(ids, ids, gate, x, shared, g.reshape(1, d), b.reshape(1, d), y_slabs)


def _moe_tables(counts, n_j, nb_max, bm):
    n_exp = counts.shape[0]
    nb = (counts + bm - 1) // bm
    blk_end = jnp.cumsum(nb)
    blk_start = blk_end - nb
    total = blk_end[-1]
    i = jnp.arange(n_j * nb_max, dtype=I32)
    valid = i < n_j * total
    e = jnp.minimum(jnp.sum(i[:, None] >= n_j * blk_end[None, :], axis=1), n_exp - 1).astype(I32)
    local = i - n_j * blk_start[e]
    nbe = jnp.maximum(nb[e], 1)
    j = local // nbe
    rb = local % nbe
    r = blk_start[e] + rb
    last = jnp.maximum(n_j * total - 1, 0)
    fix = lambda a: jnp.where(valid, a, a[last]).astype(I32)
    first = ((rb == 0) & valid).astype(I32)
    return (fix(e), fix(j), fix(r), jnp.zeros_like(i), first, valid.astype(I32)), blk_start, total


def _moe_ffn(xf, xb, x_slabs, l, p, ln_g, ln_b, alpha):
    m, d = xf.shape
    spr = d // (2 * LANES)
    n_exp = p["router_w"].shape[-1]
    f = p["w_down"].shape[2]
    bm = MOE_BM
    nb_max = (m * TOP_K) // bm + n_exp
    eid, pos, gate, counts = _router(xf, p["router_w"][l], p["router_b"][l])
    cn = _pick(f, (384, 256, 128))
    up_tables, blk_start, total = _moe_tables(counts, f // cn, nb_max, bm)
    dn_tables, _, _ = _moe_tables(counts, 1, nb_max, bm)
    dest = bm * blk_start[eid] + pos
    tok = jnp.broadcast_to(jnp.arange(m, dtype=I32)[:, None], (m, TOP_K))
    row_tok = jnp.zeros((nb_max * bm,), I32).at[dest.reshape(-1)].set(
        tok.reshape(-1), unique_indices=True)
    xs = _gather_rows(x_slabs.reshape(m, spr, LANES), row_tok, total.astype(I32), bm)
    w_gu = p["w_gate_up"].reshape(-1, d, 2 * f)
    w_dn = p["w_down"].reshape(-1, f, d)
    h = _gmm(up_tables, xs, w_gu, name="moe_up", tm=bm, tn=cn, n_cols=f, col0=0, col0_up=f,
             out_dtype=BF16, e0=l * n_exp)[0]
    y = _gmm(dn_tables, h, w_dn, name="moe_down", tm=bm, tn=d, n_cols=d, e0=l * n_exp, slabs=True)[0]
    y = y.reshape(nb_max * bm, spr, LANES)
    hs = _dense(xb, p["shared_gate_up"], l, name="shared_up", n_cols=f, col0=0, col0_up=f,
                out_dtype=BF16, tn=cn)
    ys = _dense(hs, p["shared_down"], l, name="shared_down", n_cols=d, out_dtype=F32,
                tn=_pick(d, (1024, 512, 256, 128)))
    return _combine_ln(dest, gate, xf, ys, y, ln_g, ln_b, alpha)


def _sgu_body(hu_ref, hv_ref, g_ref, b_ref, wm_ref, bias_ref, o_ref, v_ref, *, n_groups):
    v = _layer_norm(hv_ref[...], g_ref[...], b_ref[...])
    v_ref[...] = v
    vb = v.astype(BF16)
    gd = v.shape[1] // n_groups
    for g in range(n_groups):
        cols = slice(g * gd, (g + 1) * gd)
        mixed = jnp.dot(wm_ref[0, g].astype(BF16), vb[:, cols], preferred_element_type=F32)
        mixed = mixed + bias_ref[0][:, g:g + 1]
        o_ref[:, cols] = (hu_ref[:, cols].astype(F32) * mixed).astype(BF16)


def _sgu(hu, hv, g, b, wmix, bias, n_prompt_blocks):
    m, dt = hv.shape
    c = wmix.shape[-1]
    ng = wmix.shape[1]
    nblk = m // c
    npb = n_prompt_blocks
    row = pl.BlockSpec((c, dt), lambda i: (i, 0))
    vec = pl.BlockSpec((1, dt), lambda i: (0, 0))
    kind = lambda i: jnp.where(i >= npb, 1, 0)
    return pl.pallas_call(
        functools.partial(_sgu_body, n_groups=ng),
        grid=(nblk,),
        in_specs=[row, row, vec, vec,
                  pl.BlockSpec((1, ng, c, c), lambda i: (kind(i), 0, 0, 0)),
                  pl.BlockSpec((1, c, ng), lambda i: (kind(i), 0, 0))],
        out_specs=[row, pl.BlockSpec((c, dt), lambda i: (jnp.maximum(i - npb, 0), 0))],
        out_shape=[jax.ShapeDtypeStruct((m, dt), BF16),
                   jax.ShapeDtypeStruct(((nblk - npb) * c, dt), F32)],
        compiler_params=_cparams(), name="spatial_gate",
    )(hu, hv, g.reshape(1, dt), b.reshape(1, dt), wmix, bias)


def _softmax_rows(s):
    mx = jnp.max(s, axis=-1, keepdims=True)
    p = jnp.exp(s - mx)
    return p / jnp.sum(p, axis=-1, keepdims=True)


def _mem_heads(q, k_ref, v_ref, n_heads):
    dh = q.shape[1] // n_heads
    outs = []
    for hh in range(n_heads):
        cols = slice(hh * dh, (hh + 1) * dh)
        s = lax.dot_general(q[:, cols], k_ref[:, cols].astype(BF16), (((1,), (1,)), ((), ())),
                            preferred_element_type=F32) * dh ** -0.5
        p = _softmax_rows(s).astype(BF16)
        outs.append(jnp.dot(p, v_ref[:, cols].astype(BF16), preferred_element_type=F32))
    return jnp.concatenate(outs, axis=1)


def _mem_prompt_body(q_ref, k_ref, v_ref, o_ref, *, n_heads):
    o_ref[...] = _mem_heads(q_ref[...], k_ref.at[0, 0], v_ref.at[0, 0], n_heads).astype(BF16)


def _mem_attn_prompt(q, mem_k, mem_v, l, n_heads, n_batch, seq):
    dq = q.shape[1]
    mlen = mem_k.shape[2]
    tq = _pick(seq, (512, 256, 128))
    nt = seq // tq
    kv = pl.BlockSpec((1, 1, mlen, dq), lambda n, i: (l, n, 0, 0))
    return pl.pallas_call(
        functools.partial(_mem_prompt_body, n_heads=n_heads),
        grid=(n_batch, nt),
        in_specs=[pl.BlockSpec((tq, dq), lambda n, i: (n * nt + i, 0)), kv, kv],
        out_specs=pl.BlockSpec((tq, dq), lambda n, i: (n * nt + i, 0)),
        out_shape=jax.ShapeDtypeStruct((n_batch * seq, dq), BF16),
        compiler_params=_cparams(2), name="mem_attn_prompt",
    )(q, mem_k, mem_v)


def _mem_sample_body(q_ref, k_ref, v_ref, o_ref, *, n_heads, seq):
    q = q_ref[...]
    rows = lax.broadcasted_iota(I32, (q.shape[0], 1), 0)
    out = jnp.zeros(q.shape, F32)
    for s in range(q.shape[0] // seq):
        o = _mem_heads(q, k_ref.at[0, s], v_ref.at[0, s], n_heads)
        out = jnp.where((rows >= s * seq) & (rows < (s + 1) * seq), o, out)
    o_ref[...] = out.astype(BF16)


def _mem_attn_sample(q, mem_k, mem_v, l, n_heads, row0, seq):
    dq = q.shape[1]
    n_seq, mlen = mem_k.shape[1], mem_k.shape[2]
    per = max(16 // seq, 1)
    tq = per * seq
    assert row0 % tq == 0 and n_seq % per == 0
    kv = pl.BlockSpec((1, per, mlen, dq), lambda i: (l, i, 0, 0))
    return pl.pallas_call(
        functools.partial(_mem_sample_body, n_heads=n_heads, seq=seq),
        grid=(n_seq // per,),
        in_specs=[pl.BlockSpec((tq, dq), lambda i: (row0 // tq + i, 0)), kv, kv],
        out_specs=pl.BlockSpec((tq, dq), lambda i: (i, 0)),
        out_shape=jax.ShapeDtypeStruct((n_seq * seq, dq), BF16),
        compiler_params=_cparams(), name="mem_attn_sample",
    )(q, mem_k, mem_v)


def _sink_attend(s, mask, sink, vb):
    s = jnp.where(mask, s, -jnp.inf)
    mx = jnp.maximum(jnp.max(s, axis=-1, keepdims=True), sink)
    p = jnp.exp(s - mx)
    den = jnp.sum(p, axis=-1, keepdims=True) + jnp.exp(sink - mx)
    return jnp.dot((p / den).astype(BF16), vb, preferred_element_type=F32)


def _win_prompt_body(sink_ref, q_ref, kvp_ref, kvc_ref, o_ref, *, n_kv, q_per_kv, dh, sink_row):
    b = pl.program_id(1)
    w = q_ref.shape[0]
    kvw = n_kv * dh
    qi = lax.broadcasted_iota(I32, (w, 2 * w), 0)
    kj = lax.broadcasted_iota(I32, (w, 2 * w), 1)
    mask = (kj > qi) & (kj <= qi + w) & ((kj >= w) | (b > 0))
    outs = []
    for h in range(n_kv):
        kc = slice(h * dh, (h + 1) * dh)
        vc = slice(kvw + h * dh, kvw + (h + 1) * dh)
        kb = jnp.concatenate([kvp_ref[:, kc], kvc_ref[:, kc]], axis=0).astype(BF16)
        vb = jnp.concatenate([kvp_ref[:, vc], kvc_ref[:, vc]], axis=0).astype(BF16)
        for g in range(q_per_kv):
            hq = h * q_per_kv + g
            s = lax.dot_general(q_ref[:, hq * dh:(hq + 1) * dh], kb, (((1,), (1,)), ((), ())),
                                preferred_element_type=F32) * dh ** -0.5
            outs.append(_sink_attend(s, mask, sink_ref[sink_row, hq], vb))
    o_ref[...] = jnp.concatenate(outs, axis=1).astype(BF16)


def _win_attn_prompt(q, kv, sinks, jl, n_batch, seq, n_kv, dh):
    dq = q.shape[1]
    q_per_kv = dq // (n_kv * dh)
    nb = seq // WINDOW
    kvw = kv.shape[1]
    grid_spec = pltpu.PrefetchScalarGridSpec(
        num_scalar_prefetch=1, grid=(n_batch, nb),
        in_specs=[pl.BlockSpec((WINDOW, dq), lambda n, b, s: (n * nb + b, 0)),
                  pl.BlockSpec((WINDOW, kvw), lambda n, b, s: (n * nb + jnp.maximum(b - 1, 0), 0)),
                  pl.BlockSpec((WINDOW, kvw), lambda n, b, s: (n * nb + b, 0))],
        out_specs=pl.BlockSpec((WINDOW, dq), lambda n, b, s: (n * nb + b, 0)))
    return pl.pallas_call(
        functools.partial(_win_prompt_body, n_kv=n_kv, q_per_kv=q_per_kv, dh=dh, sink_row=jl),
        grid_spec=grid_spec,
        out_shape=jax.ShapeDtypeStruct((n_batch * seq, dq), BF16),
        compiler_params=_cparams(2), name="win_attn_prompt",
    )(sinks, q, kv, kv)


def _win_sample_body(q_ref, k_ref, v_ref, sink_ref, o_ref, *, n_kv, seq, n_buf):
    rows = q_ref.shape[2]
    nk = k_ref.shape[2]
    assert seq & (seq - 1) == 0
    t = lax.broadcasted_iota(I32, (rows, nk), 0) & (seq - 1)
    kj = lax.broadcasted_iota(I32, (rows, nk), 1)
    diff = t - (kj - n_buf)
    mask = (diff >= 0) & (diff < WINDOW)
    dh = q_ref.shape[3]
    for h in range(n_kv):
        s = lax.dot_general(q_ref[0, h].astype(BF16), k_ref[0, h].astype(BF16),
                            (((1,), (1,)), ((), ())), preferred_element_type=F32) * dh ** -0.5
        o_ref[0, h] = _sink_attend(s, mask, sink_ref[0, h], v_ref[0, h].astype(BF16))


def _win_attn_sample(q, kk, vv, sink_rows, seq, n_buf):
    n, n_kv, rows, dh = q.shape
    nk = kk.shape[2]
    blk = lambda r: pl.BlockSpec((1, n_kv, r, dh), lambda i: (i, 0, 0, 0))
    return pl.pallas_call(
        functools.partial(_win_sample_body, n_kv=n_kv, seq=seq, n_buf=n_buf),
        grid=(n,),
        in_specs=[blk(rows), blk(nk), blk(nk),
                  pl.BlockSpec((1, n_kv, rows, 1), lambda i: (0, 0, 0, 0))],
        out_specs=blk(rows),
        out_shape=jax.ShapeDtypeStruct((n, n_kv, rows, dh), F32),
        compiler_params=_cparams(), name="win_attn_sample",
    )(q, kk, vv, sink_rows)


def kernel(x_prompt, x_sample, mem_prompt, cache_mem_k, cache_mem_v, cache_win_k, cache_win_v, w_in_a, sgu_g, sgu_b, w_spatial, b_spatial, w_in_b, w_kv, sinks, w_mem_kv, w_out, ln_g, ln_b, router_w, router_b, w_gate_up, w_down, shared_gate_up, shared_down):
    nb, seq, d = x_prompt.shape
    ns, sseq, _ = x_sample.shape
    depth = w_out.shape[0]
    n_a = w_in_a.shape[0]
    mem_len = mem_prompt.shape[1]
    mem_heads, mem_dh = cache_mem_k.shape[3], cache_mem_k.shape[4]
    d_mem = mem_heads * mem_dh
    d_tok = d - d_mem
    n_buf, n_kv, dh = cache_win_k.shape[1], cache_win_k.shape[2], cache_win_k.shape[3]
    kvw = n_kv * dh
    q_per_kv = d_tok // kvw
    chunk = w_spatial.shape[2]
    n_groups = w_spatial.shape[1]
    alpha = float((2 * depth) ** 0.25)
    mp, ms = nb * seq, ns * sseq
    m = mp + ms
    p = dict(router_w=router_w, router_b=router_b, w_gate_up=w_gate_up, w_down=w_down,
             shared_gate_up=shared_gate_up, shared_down=shared_down)

    memb = mem_prompt.reshape(nb * mem_len, d).astype(BF16)
    tm_mem = _pick(nb * mem_len, (512, 256, 128))
    tn_mem = _pick(d_mem, (512, 256, 128))
    n_j, n_r = d_mem // tn_mem, nb * mem_len // tm_mem
    lay = np.repeat(np.arange(depth), n_j * n_r)
    jj = np.tile(np.repeat(np.arange(n_j), n_r), depth)
    rr = np.tile(np.arange(n_r), depth * n_j)
    mem_tables = tuple(jnp.asarray(c, I32) for c in
                       (lay, jj, rr, lay, (rr == 0).astype(np.int64), np.ones_like(lay)))
    mem_kw = dict(tm=tm_mem, tn=tn_mem, n_cols=d_mem, out_dtype=F32, n_out=depth)
    pmem_k = _gmm(mem_tables, memb, w_mem_kv, name="mem_k_proj", col0=0, **mem_kw)
    pmem_v = _gmm(mem_tables, memb, w_mem_kv, name="mem_v_proj", col0=d_mem, **mem_kw)
    pmem_k = pmem_k.reshape(depth, nb, mem_len, d_mem)
    pmem_v = pmem_v.reshape(depth, nb, mem_len, d_mem)
    smem_k = cache_mem_k.reshape(depth, ns, mem_len, d_mem)
    smem_v = cache_mem_v.reshape(depth, ns, mem_len, d_mem)

    xf = jnp.concatenate([x_prompt.reshape(mp, d), x_sample.reshape(ms, d)], axis=0)
    xb = xf.astype(BF16)

    tril = jnp.tril(jnp.ones((chunk, chunk), F32))
    c_s = min(sseq, chunk)
    reps = chunk // c_s
    eye = jnp.eye(reps, dtype=F32)
    w_p = w_spatial * tril
    w_c = (w_spatial[:, :, :c_s, :c_s] * tril[:c_s, :c_s])
    w_s = jnp.einsum("ab,lgts->lgatbs", eye, w_c).reshape(n_a, n_groups, chunk, chunk)
    wmix = jnp.stack([w_p, w_s], axis=1)
    bias_p = jnp.swapaxes(b_spatial, 1, 2)
    bias_s = jnp.tile(bias_p[:, :c_s], (1, reps, 1))
    bias = jnp.stack([bias_p, bias_s], axis=1)

    chunk_v = []
    kv = k_new = v_new = kk = vv = None
    for l in range(depth):
        if l < n_a:
            hu = _dense(xb, w_in_a, l, name="in_a_u", n_cols=d_tok, col0=0, act="gelu", out_dtype=BF16)
            hv = _dense(xb, w_in_a, l, name="in_a_v", n_cols=d_tok, col0=d_tok, act="gelu", out_dtype=F32)
            qm = _dense(xb, w_in_a, l, name="in_a_qmem", n_cols=d_mem, col0=2 * d_tok, out_dtype=BF16)
            tok, v_s = _sgu(hu, hv, sgu_g[l], sgu_b[l], wmix[l], bias[l], mp // chunk)
            chunk_v.append(v_s.reshape(ns, sseq, d_tok))
            tok_p, tok_s = tok[:mp], tok[mp:]
        else:
            jl = l - n_a
            if kv is None:
                kv = _dense(xb, w_kv.reshape(1, d, 2 * kvw), 0, name="kv_proj", n_cols=2 * kvw,
                            out_dtype=F32, tn=_pick(2 * kvw, (384, 256, 128)))
                k_new = kv[:, :kvw]
                v_new = kv[:, kvw:]
                kk = jnp.concatenate([cache_win_k, k_new[mp:].reshape(ns, sseq, n_kv, dh)], axis=1)
                vv = jnp.concatenate([cache_win_v, v_new[mp:].reshape(ns, sseq, n_kv, dh)], axis=1)
            q = _dense(xb, w_in_b, jl, name="in_b_q", n_cols=d_tok, col0=0, out_dtype=BF16)
            qm = _dense(xb, w_in_b, jl, name="in_b_qmem", n_cols=d_mem, col0=d_tok, out_dtype=BF16)
            tok_p = _win_attn_prompt(q, kv, sinks, jl, nb, seq, n_kv, dh)
            q_s = q[mp:].astype(F32).reshape(ns, sseq, n_kv, q_per_kv, dh)
            q_s = q_s.transpose(0, 2, 3, 1, 4).reshape(ns, n_kv, q_per_kv * sseq, dh)
            sink_rows = jnp.repeat(sinks[jl].reshape(1, n_kv, q_per_kv), sseq, axis=2)[..., None]
            o_s = _win_attn_sample(q_s, kk.transpose(0, 2, 1, 3), vv.transpose(0, 2, 1, 3),
                                   sink_rows, sseq, n_buf)
            tok_s = o_s.reshape(ns, n_kv, q_per_kv, sseq, dh).transpose(0, 3, 1, 2, 4)
            tok_s = tok_s.reshape(ms, d_tok).astype(BF16)
        mo_p = _mem_attn_prompt(qm, pmem_k, pmem_v, l, mem_heads, nb, seq)
        mo_s = _mem_attn_sample(qm, smem_k, smem_v, l, mem_heads, mp, sseq)
        cat = jnp.concatenate([jnp.concatenate([tok_p, mo_p], axis=1),
                               jnp.concatenate([tok_s, mo_s], axis=1)], axis=0)
        mix = _dense(cat, w_out, l, name="out_proj", n_cols=d, out_dtype=F32)
        xf, xb, x_slabs = _resid_ln(xf, mix, ln_g[l, 0], ln_b[l, 0], alpha)
        xf, xb = _moe_ffn(xf, xb, x_slabs, l, p, ln_g[l, 1], ln_b[l, 1], alpha)

    y_prompt = xf[:mp].reshape(nb, seq, d)
    y_sample = xf[mp:].reshape(ns, sseq, d)
    mshape = (depth, nb, mem_len, mem_heads, mem_dh)
    keep_p = min(WINDOW, seq)
    k_p = k_new[:mp].reshape(nb, seq, n_kv, dh)[:, -keep_p:]
    v_p = v_new[:mp].reshape(nb, seq, n_kv, dh)[:, -keep_p:]
    return (y_prompt, y_sample, pmem_k.reshape(mshape), pmem_v.reshape(mshape), k_p, v_p,
            kk[:, -n_buf:], vv[:, -n_buf:], jnp.stack(chunk_v))
```

```python
import functools

import numpy as np
import jax
import jax.numpy as jnp
from jax import lax
from jax.experimental import pallas as pl
from jax.experimental.pallas import tpu as pltpu

F32, BF16, I32, U32 = jnp.float32, jnp.bfloat16, jnp.int32, jnp.uint32

LANES = 128
TOP_K = 8
N_GROUPS = 8
TOPK_GROUPS = 4
ROUTED_SCALE = 2.5
WINDOW = 128
LN_EPS = 1e-5
MOE_BM = 256
VMEM_LIMIT = 56 * 1024 * 1024


def _pick(n, prefs):
    for p in prefs:
        if n % p == 0:
            return p
    return n


def _cparams(n_axes=1):
    return pltpu.CompilerParams(dimension_semantics=("arbitrary",) * n_axes,
                                vmem_limit_bytes=VMEM_LIMIT)


def _layer_norm(xf, g, b):
    mu = jnp.mean(xf, axis=-1, keepdims=True)
    var = jnp.mean(jnp.square(xf - mu), axis=-1, keepdims=True)
    return (xf - mu) * lax.rsqrt(var + LN_EPS) * g + b


def _pack_rows(a):
    half = a.shape[1] // 2
    lo = pltpu.bitcast(a[:, :half].astype(BF16).astype(F32), U32)
    hi = pltpu.bitcast(a[:, half:].astype(BF16).astype(F32), U32)
    return (lo >> 16) | (hi & jnp.uint32(0xFFFF0000))


def _store_slabs(o_ref, u):
    t = u.shape[0]
    s = u.shape[1] // LANES
    for c in range(s):
        o_ref[pl.ds(c, t, stride=s), :] = u[:, c * LANES:(c + 1) * LANES]


def _unpack_words(u):
    lo = pltpu.bitcast(u << 16, F32)
    hi = pltpu.bitcast(u & jnp.uint32(0xFFFF0000), F32)
    return lo, hi


def _cast_block(src_ref, dst_ref, col0):
    k, tn = src_ref.shape[1], src_ref.shape[2]
    ck = _pick(k, (512, 256, 128))

    def body(c, carry):
        rows = pl.ds(pl.multiple_of(c * ck, ck), ck)
        dst_ref[rows, col0:col0 + tn] = src_ref[0, rows, :].astype(BF16)
        return carry

    lax.fori_loop(0, k // ck, body, 0)


def _gmm_body(e_ref, j_ref, r_ref, eo_ref, first_ref, valid_ref, x_ref, *rest, gated, act, slabs):
    del e_ref, j_ref, r_ref, eo_ref
    if gated:
        w_ref, w2_ref, o_ref, wb_ref = rest
    else:
        w_ref, o_ref, wb_ref = rest
    tn = w_ref.shape[2]
    i = pl.program_id(0)

    @pl.when(first_ref[i] == 1)
    def _cast():
        _cast_block(w_ref, wb_ref, 0)
        if gated:
            _cast_block(w2_ref, wb_ref, tn)

    @pl.when(valid_ref[i] == 1)
    def _compute():
        a = jnp.dot(x_ref[...], wb_ref[...], preferred_element_type=F32)
        if gated:
            a = jax.nn.silu(a[:, :tn]) * a[:, tn:]
        elif act == "gelu":
            a = jax.nn.gelu(a)
        if slabs:
            _store_slabs(o_ref.at[0], _pack_rows(a))
        else:
            o_ref[0] = a.astype(o_ref.dtype)


def _gmm(tables, x, w, *, name, tm, tn, n_cols, col0=0, col0_up=None, act=None, out_dtype=F32,
         n_out=1, e0=0, slabs=False):
    m, k = x.shape
    gated = col0_up is not None
    n_items = tables[0].shape[0]
    assert m % tm == 0 and n_cols % tn == 0 and col0 % tn == 0
    if slabs:
        assert tn == n_cols and n_cols % (2 * LANES) == 0
        spr = n_cols // (2 * LANES)
        out_spec = pl.BlockSpec((1, tm * spr, LANES), lambda i, e, j, r, eo, f, v: (eo[i], r[i], 0))
        out_shape = jax.ShapeDtypeStruct((n_out, m * spr, LANES), U32)
    else:
        out_spec = pl.BlockSpec((1, tm, tn), lambda i, e, j, r, eo, f, v: (eo[i], r[i], j[i]))
        out_shape = jax.ShapeDtypeStruct((n_out, m, n_cols), out_dtype)
    jb0 = col0 // tn
    in_specs = [
        pl.BlockSpec((tm, k), lambda i, e, j, r, eo, f, v: (r[i], 0)),
        pl.BlockSpec((1, k, tn), lambda i, e, j, r, eo, f, v: (e0 + e[i], 0, jb0 + j[i])),
    ]
    operands = [x, w]
    if gated:
        assert col0_up % tn == 0 and tn % LANES == 0
        jb1 = col0_up // tn
        in_specs.append(pl.BlockSpec((1, k, tn), lambda i, e, j, r, eo, f, v: (e0 + e[i], 0, jb1 + j[i])))
        operands.append(w)
    scratch = [pltpu.VMEM((k, 2 * tn if gated else tn), BF16)]
    grid_spec = pltpu.PrefetchScalarGridSpec(
        num_scalar_prefetch=6, grid=(n_items,), in_specs=in_specs, out_specs=out_spec,
        scratch_shapes=scratch)
    return pl.pallas_call(
        functools.partial(_gmm_body, gated=gated, act=act, slabs=slabs),
        grid_spec=grid_spec, out_shape=out_shape, compiler_params=_cparams(), name=name,
    )(*tables, *operands)


def _dense_tables(n_j, n_r, e=0):
    j = np.repeat(np.arange(n_j), n_r)
    r = np.tile(np.arange(n_r), n_j)
    z = np.zeros_like(j)
    cols = (z + e, j, r, z, (r == 0).astype(np.int64), z + 1)
    return tuple(jnp.asarray(c, I32) for c in cols)


def _dense(x, w, e, *, name, n_cols, col0=0, col0_up=None, act=None, out_dtype=F32, tm=None, tn=None):
    m = x.shape[0]
    tm = tm or _pick(m, (768, 512, 256, 128))
    tn = tn or _pick(n_cols, (512, 384, 256, 128))
    tables = _dense_tables(n_cols // tn, m // tm, e)
    return _gmm(tables, x, w, name=name, tm=tm, tn=tn, n_cols=n_cols, col0=col0, col0_up=col0_up,
                act=act, out_dtype=out_dtype)[0]


def _resid_ln_body(x_ref, y_ref, g_ref, b_ref, o_ref, ob_ref, os_ref, *, alpha):
    o = _layer_norm(alpha * x_ref[...] + y_ref[...], g_ref[...], b_ref[...])
    o_ref[...] = o
    ob_ref[...] = o.astype(BF16)
    _store_slabs(os_ref, _pack_rows(o))


def _resid_ln(x, y, g, b, alpha):
    m, d = x.shape
    tm = _pick(m, (256, 128))
    spr = d // (2 * LANES)
    row = pl.BlockSpec((tm, d), lambda i: (i, 0))
    vec = pl.BlockSpec((1, d), lambda i: (0, 0))
    return pl.pallas_call(
        functools.partial(_resid_ln_body, alpha=alpha),
        grid=(m // tm,), in_specs=[row, row, vec, vec],
        out_specs=[row, row, pl.BlockSpec((tm * spr, LANES), lambda i: (i, 0))],
        out_shape=[jax.ShapeDtypeStruct((m, d), F32), jax.ShapeDtypeStruct((m, d), BF16),
                   jax.ShapeDtypeStruct((m * spr, LANES), U32)],
        compiler_params=_cparams(), name="resid_ln",
    )(x, y, g.reshape(1, d), b.reshape(1, d))


def _router_body(x_ref, rw_ref, rb_ref, eid_ref, pos_ref, gate_ref, cnt_ref, base_ref, *, n_exp):
    tm = x_ref.shape[0]
    gsz = n_exp // N_GROUPS
    i = pl.program_id(0)

    @pl.when(i == 0)
    def _init():
        base_ref[...] = jnp.zeros_like(base_ref)

    logits = jnp.dot(x_ref[...], rw_ref[...], precision=lax.Precision.HIGHEST,
                     preferred_element_type=F32)
    s = jax.nn.sigmoid(logits.T[:n_exp])
    sb = s + rb_ref[...]
    row = lax.broadcasted_iota(I32, (n_exp, tm), 0)
    assert gsz & (gsz - 1) == 0
    grp = lax.shift_right_logical(row, gsz.bit_length() - 1)
    bcast = lambda a, e2: jnp.broadcast_to(a[e2:e2 + 1, :], (n_exp, tm))

    def count_beaten(a, idx, same=None):
        n = jnp.zeros((n_exp, tm), I32)
        for e2 in range(n_exp):
            c = bcast(a, e2)
            beats = (c > a) | ((c == a) & (e2 < idx))
            if same is not None:
                beats = beats & (same == e2 // gsz)
            n = n + jnp.where(beats, 1, 0)
        return n

    top2 = jnp.where(count_beaten(sb, row, grp) < 2, sb, 0.0)
    gsum = jnp.zeros((n_exp, tm), F32)
    for e2 in range(n_exp):
        gsum = gsum + jnp.where(grp == e2 // gsz, bcast(top2, e2), 0.0)
    n_beat = jnp.zeros((n_exp, tm), I32)
    for g2 in range(N_GROUPS):
        c = bcast(gsum, g2 * gsz)
        n_beat = n_beat + jnp.where((c > gsum) | ((c == gsum) & (g2 < grp)), 1, 0)
    v = jnp.where(n_beat < TOPK_GROUPS, sb, -jnp.inf)
    sel = count_beaten(v, row) < TOP_K
    w = jnp.where(sel, s, 0.0)
    gate = w / jnp.sum(w, axis=0, keepdims=True) * ROUTED_SCALE

    sel_b = jnp.where(sel, 1.0, 0.0).astype(BF16)
    ti = lax.broadcasted_iota(I32, (tm, tm), 0)
    tj = lax.broadcasted_iota(I32, (tm, tm), 1)
    earlier = jnp.where(ti < tj, 1.0, 0.0).astype(BF16)
    pos = base_ref[:, 0:1] + jnp.dot(sel_b, earlier, preferred_element_type=F32)
    base_ref[...] = base_ref[...] + jnp.sum(sel_b.astype(F32), axis=1, keepdims=True)
    cnt_ref[...] = base_ref[...].astype(I32)

    ei = lax.broadcasted_iota(I32, (n_exp, n_exp), 0)
    ej = lax.broadcasted_iota(I32, (n_exp, n_exp), 1)
    upto = jnp.where(ej <= ei, 1.0, 0.0).astype(BF16)
    slot = jnp.dot(upto, sel_b, preferred_element_type=F32)
    row_f = row.astype(F32)
    for k in range(TOP_K):
        m = sel & (slot == float(k + 1))
        eid_ref[k:k + 1, :] = jnp.sum(jnp.where(m, row_f, 0.0), axis=0, keepdims=True).astype(I32)
        pos_ref[k:k + 1, :] = jnp.sum(jnp.where(m, pos, 0.0), axis=0, keepdims=True).astype(I32)
        gate_ref[k:k + 1, :] = jnp.sum(jnp.where(m, gate, 0.0), axis=0, keepdims=True)


def _router(x, rw, rb):
    m, d = x.shape
    n_exp = rw.shape[1]
    n_pad = -(-n_exp // LANES) * LANES
    tm = _pick(m, (256, 128))
    slot = pl.BlockSpec((TOP_K, tm), lambda i: (0, i))
    eid, pos, gate, cnt = pl.pallas_call(
        functools.partial(_router_body, n_exp=n_exp),
        grid=(m // tm,),
        in_specs=[pl.BlockSpec((tm, d), lambda i: (i, 0)),
                  pl.BlockSpec((d, n_pad), lambda i: (0, 0)),
                  pl.BlockSpec((n_exp, 1), lambda i: (0, 0))],
        out_specs=[slot, slot, slot, pl.BlockSpec((n_exp, LANES), lambda i: (0, 0))],
        out_shape=[jax.ShapeDtypeStruct((TOP_K, m), I32), jax.ShapeDtypeStruct((TOP_K, m), I32),
                   jax.ShapeDtypeStruct((TOP_K, m), F32), jax.ShapeDtypeStruct((n_exp, LANES), I32)],
        scratch_shapes=[pltpu.VMEM((n_exp, LANES), F32)],
        compiler_params=_cparams(), name="router",
    )(x, jnp.pad(rw, ((0, 0), (0, n_pad - n_exp))), rb.reshape(n_exp, 1))
    return eid.T, pos.T, gate.T, cnt[:, 0]


def _gather_body(nblk_ref, tok_ref, nxt_ref, x_hbm, o_ref, buf, sem):
    i = pl.program_id(0)
    nblk = nblk_ref[0]
    spr = x_hbm.shape[1]
    bm = buf.shape[0] // (2 * spr)
    slot = lax.rem(i, 2)

    def row_copy(ids_ref, s, r):
        dst = buf.at[pl.ds(pl.multiple_of((s * bm + r) * spr, spr), spr), :]
        return pltpu.make_async_copy(x_hbm.at[ids_ref[0, 0, r]], dst, sem.at[s])

    def start_all(ids_ref, s):
        def body(h, c):
            for u in range(2):
                row_copy(ids_ref, s, 2 * h + u).start(priority=u)
            return c

        lax.fori_loop(0, bm // 2, body, 0, unroll=4)

    @pl.when(i == 0)
    def _first():
        start_all(tok_ref, 0)

    @pl.when(i + 1 < nblk)
    def _prefetch():
        start_all(nxt_ref, 1 - slot)

    @pl.when(i < nblk)
    def _block():
        def wait(r, c):
            row_copy(tok_ref, slot, r).wait()
            return c

        lax.fori_loop(0, bm, wait, 0, unroll=8)
        base = slot * (bm * spr)
        los, his = [], []
        for c in range(spr):
            lo, hi = _unpack_words(buf[pl.ds(base + c, bm, stride=spr), :])
            los.append(lo.astype(BF16))
            his.append(hi.astype(BF16))
        o_ref[...] = jnp.concatenate(los + his, axis=1)


def _gather_rows(x_slabs, row_tok, n_blocks, bm):
    nb_max = row_tok.shape[0] // bm
    _, spr, _ = x_slabs.shape
    d = spr * 2 * LANES
    ids = row_tok.reshape(nb_max, 1, bm)
    grid_spec = pltpu.PrefetchScalarGridSpec(
        num_scalar_prefetch=1, grid=(nb_max,),
        in_specs=[pl.BlockSpec((1, 1, bm), lambda i, nb: (jnp.minimum(i, nb[0] - 1), 0, 0),
                               memory_space=pltpu.SMEM),
                  pl.BlockSpec((1, 1, bm), lambda i, nb: (jnp.minimum(i + 1, nb[0] - 1), 0, 0),
                               memory_space=pltpu.SMEM),
                  pl.BlockSpec(memory_space=pl.ANY)],
        out_specs=pl.BlockSpec((bm, d), lambda i, nb: (jnp.minimum(i, nb[0] - 1), 0)),
        scratch_shapes=[pltpu.VMEM((2 * bm * spr, LANES), U32), pltpu.SemaphoreType.DMA((2,))])
    return pl.pallas_call(
        _gather_body, grid_spec=grid_spec,
        out_shape=jax.ShapeDtypeStruct((nb_max * bm, d), BF16),
        compiler_params=_cparams(), name="moe_gather",
    )(n_blocks.reshape(1), ids, ids, x_slabs)


def _combine_body(dest_ref, nxt_ref, gate_ref, x_ref, sh_ref, g_ref, b_ref, y_hbm, o_ref, ob_ref,
                  buf, sem, *, alpha):
    i = pl.program_id(0)
    tt = x_ref.shape[0]
    spr = y_hbm.shape[1]
    slot = lax.rem(i, 2)

    def row_copy(ids_ref, s, t, k):
        off = pl.multiple_of(((s * TOP_K + k) * tt + t) * spr, spr)
        return pltpu.make_async_copy(y_hbm.at[ids_ref[0, 0, t * TOP_K + k]],
                                     buf.at[pl.ds(off, spr), :], sem.at[s])

    def start_all(ids_ref, s):
        def body(t, c):
            for k in range(TOP_K):
                row_copy(ids_ref, s, t, k).start(priority=k % 2)
            return c

        lax.fori_loop(0, tt, body, 0)

    @pl.when(i == 0)
    def _first():
        start_all(dest_ref, 0)

    @pl.when(i + 1 < pl.num_programs(0))
    def _prefetch():
        start_all(nxt_ref, 1 - slot)

    acc = alpha * x_ref[...] + sh_ref[...]

    def wait(t, c):
        for k in range(TOP_K):
            row_copy(dest_ref, slot, t, k).wait()
        return c

    lax.fori_loop(0, tt, wait, 0)
    los, his = [None] * spr, [None] * spr
    for k in range(TOP_K):
        gk = gate_ref[:, k:k + 1]
        base = (slot * TOP_K + k) * (tt * spr)
        for c in range(spr):
            lo, hi = _unpack_words(buf[pl.ds(base + c, tt, stride=spr), :])
            los[c] = gk * lo if k == 0 else los[c] + gk * lo
            his[c] = gk * hi if k == 0 else his[c] + gk * hi
    routed = jnp.concatenate(los + his, axis=1)
    o = _layer_norm(acc + routed, g_ref[...], b_ref[...])
    o_ref[...] = o
    ob_ref[...] = o.astype(BF16)


def _combine_ln(dest, gate, x, shared, y_slabs, g, b, alpha):
    m, d = x.shape
    spr = y_slabs.shape[1]
    tt = _pick(m, (64, 32, 16, 8))
    n = m // tt
    row = pl.BlockSpec((tt, d), lambda i: (i, 0))
    vec = pl.BlockSpec((1, d), lambda i: (0, 0))
    ids = dest.reshape(n, 1, tt * TOP_K)
    return pl.pallas_call(
        functools.partial(_combine_body, alpha=alpha),
        grid=(n,),
        in_specs=[pl.BlockSpec((1, 1, tt * TOP_K), lambda i: (i, 0, 0), memory_space=pltpu.SMEM),
                  pl.BlockSpec((1, 1, tt * TOP_K), lambda i: (jnp.minimum(i + 1, n - 1), 0, 0),
                               memory_space=pltpu.SMEM),
                  pl.BlockSpec((tt, TOP_K), lambda i: (i, 0)), row, row, vec, vec,
                  pl.BlockSpec(memory_space=pl.ANY)],
        out_specs=[row, row],
        out_shape=[jax.ShapeDtypeStruct((m, d), F32), jax.ShapeDtypeStruct((m, d), BF16)],
        scratch_shapes=[pltpu.VMEM((2 * TOP_K * tt * spr, LANES), U32), pltpu.SemaphoreType.DMA((2,))],
        compiler_params=_cparams(), name="moe_combine",
    )(ids, ids, gate, x, shared, g.reshape(1, d), b.reshape(1, d), y_slabs)


def _moe_tables(counts, n_j, nb_max, bm):
    n_exp = counts.shape[0]
    nb = (counts + bm - 1) // bm
    blk_end = jnp.cumsum(nb)
    blk_start = blk_end - nb
    total = blk_end[-1]
    i = jnp.arange(n_j * nb_max, dtype=I32)
    valid = i < n_j * total
    ids = jnp.arange(n_exp, dtype=I32)
    own = (i[:, None] >= n_j * blk_start[None, :]) & (i[:, None] < n_j * blk_end[None, :])
    pick = lambda a: jnp.sum(jnp.where(own, a[None, :], 0), axis=1)
    e, bs, nbe = pick(ids), pick(blk_start), jnp.maximum(pick(nb), 1)
    local = i - n_j * bs
    j = local // nbe
    rb = local - j * nbe
    r = bs + rb
    e_last = jnp.max(jnp.where(nb > 0, ids, 0))
    e = jnp.where(valid, e, e_last)
    j = jnp.where(valid, j, n_j - 1)
    r = jnp.where(valid, r, total - 1)
    first = ((rb == 0) & valid).astype(I32)
    tables = (e, j, r, jnp.zeros_like(i), first, valid)
    return tuple(t.astype(I32) for t in tables), blk_start, total


def _moe_ffn(xf, xb, x_slabs, l, p, ln_g, ln_b, alpha):
    m, d = xf.shape
    spr = d // (2 * LANES)
    n_exp = p["router_w"].shape[-1]
    f = p["w_down"].shape[2]
    bm = MOE_BM
    nb_max = (m * TOP_K) // bm + n_exp
    eid, pos, gate, counts = _router(xf, p["router_w"][l], p["router_b"][l])
    cn = _pick(f, (384, 256, 128))
    up_tables, blk_start, total = _moe_tables(counts, f // cn, nb_max, bm)
    dn_tables, _, _ = _moe_tables(counts, 1, nb_max, bm)
    start_of = jnp.sum(jnp.where(eid[..., None] == jnp.arange(n_exp, dtype=I32), blk_start, 0), axis=-1)
    dest = bm * start_of + pos
    tok = jnp.broadcast_to(jnp.arange(m, dtype=I32)[:, None], (m, TOP_K))
    row_tok = jnp.zeros((nb_max * bm,), I32).at[dest.reshape(-1)].set(
        tok.reshape(-1), unique_indices=True)
    xs = _gather_rows(x_slabs.reshape(m, spr, LANES), row_tok, total.astype(I32), bm)
    w_gu = p["w_gate_up"].reshape(-1, d, 2 * f)
    w_dn = p["w_down"].reshape(-1, f, d)
    h = _gmm(up_tables, xs, w_gu, name="moe_up", tm=bm, tn=cn, n_cols=f, col0=0, col0_up=f,
             out_dtype=BF16, e0=l * n_exp)[0]
    y = _gmm(dn_tables, h, w_dn, name="moe_down", tm=bm, tn=d, n_cols=d, e0=l * n_exp, slabs=True)[0]
    y = y.reshape(nb_max * bm, spr, LANES)
    hs = _dense(xb, p["shared_gate_up"], l, name="shared_up", n_cols=f, col0=0, col0_up=f,
                out_dtype=BF16, tn=cn)
    ys = _dense(hs, p["shared_down"], l, name="shared_down", n_cols=d, out_dtype=F32,
                tn=_pick(d, (1024, 512, 256, 128)))
    return _combine_ln(dest, gate, xf, ys, y, ln_g, ln_b, alpha)


def _sgu_body(hu_ref, hv_ref, g_ref, b_ref, wm_ref, bias_ref, o_ref, v_ref, *, n_groups):
    v = _layer_norm(hv_ref[...], g_ref[...], b_ref[...])
    v_ref[...] = v
    vb = v.astype(BF16)
    gd = v.shape[1] // n_groups
    for g in range(n_groups):
        cols = slice(g * gd, (g + 1) * gd)
        mixed = jnp.dot(wm_ref[0, g].astype(BF16), vb[:, cols], preferred_element_type=F32)
        mixed = mixed + bias_ref[0][:, g:g + 1]
        o_ref[:, cols] = (hu_ref[:, cols].astype(F32) * mixed).astype(BF16)


def _sgu(hu, hv, g, b, wmix, bias, n_prompt_blocks, d_out):
    m, dt = hv.shape
    c = wmix.shape[-1]
    ng = wmix.shape[1]
    nblk = m // c
    npb = n_prompt_blocks
    row = pl.BlockSpec((c, dt), lambda i: (i, 0))
    vec = pl.BlockSpec((1, dt), lambda i: (0, 0))
    kind = lambda i: jnp.where(i >= npb, 1, 0)
    return pl.pallas_call(
        functools.partial(_sgu_body, n_groups=ng),
        grid=(nblk,),
        in_specs=[row, row, vec, vec,
                  pl.BlockSpec((1, ng, c, c), lambda i: (kind(i), 0, 0, 0)),
                  pl.BlockSpec((1, c, ng), lambda i: (kind(i), 0, 0))],
        out_specs=[row, pl.BlockSpec((c, dt), lambda i: (jnp.maximum(i - npb, 0), 0))],
        out_shape=[jax.ShapeDtypeStruct((m, d_out), BF16),
                   jax.ShapeDtypeStruct(((nblk - npb) * c, dt), F32)],
        compiler_params=_cparams(), name="spatial_gate",
    )(hu, hv, g.reshape(1, dt), b.reshape(1, dt), wmix, bias)


def _softmax_rows(s):
    mx = jnp.max(s, axis=-1, keepdims=True)
    p = jnp.exp(s - mx)
    return p / jnp.sum(p, axis=-1, keepdims=True)


def _mem_heads(q, k_ref, v_ref, n_heads):
    dh = q.shape[1] // n_heads
    outs = []
    for hh in range(n_heads):
        cols = slice(hh * dh, (hh + 1) * dh)
        s = lax.dot_general(q[:, cols], k_ref[:, cols].astype(BF16), (((1,), (1,)), ((), ())),
                            preferred_element_type=F32) * dh ** -0.5
        p = _softmax_rows(s).astype(BF16)
        outs.append(jnp.dot(p, v_ref[:, cols].astype(BF16), preferred_element_type=F32))
    return jnp.concatenate(outs, axis=1)


def _mem_prompt_body(q_ref, k_ref, v_ref, cat_ref, o_ref, *, n_heads):
    del cat_ref
    o_ref[...] = _mem_heads(q_ref[...], k_ref.at[0, 0], v_ref.at[0, 0], n_heads).astype(BF16)


def _mem_attn_prompt(q, mem_k, mem_v, cat, l, n_heads, n_batch, seq):
    dq = q.shape[1]
    mlen = mem_k.shape[2]
    tq = _pick(seq, (512, 256, 128))
    nt = seq // tq
    cb = cat.shape[1] // dq - 1
    kv = pl.BlockSpec((1, 1, mlen, dq), lambda n, i: (l, n, 0, 0))
    return pl.pallas_call(
        functools.partial(_mem_prompt_body, n_heads=n_heads),
        grid=(n_batch, nt),
        in_specs=[pl.BlockSpec((tq, dq), lambda n, i: (n * nt + i, 0)), kv, kv,
                  pl.BlockSpec(memory_space=pl.ANY)],
        out_specs=pl.BlockSpec((tq, dq), lambda n, i: (n * nt + i, cb)),
        out_shape=jax.ShapeDtypeStruct(cat.shape, cat.dtype),
        input_output_aliases={3: 0},
        compiler_params=_cparams(2), name="mem_attn_prompt",
    )(q, mem_k, mem_v, cat)


def _mem_sample_body(q_ref, k_ref, v_ref, cat_ref, o_ref, *, n_heads, seq):
    del cat_ref
    q = q_ref[...]
    rows = lax.broadcasted_iota(I32, (q.shape[0], 1), 0)
    out = jnp.zeros(q.shape, F32)
    for s in range(q.shape[0] // seq):
        o = _mem_heads(q, k_ref.at[0, s], v_ref.at[0, s], n_heads)
        out = jnp.where((rows >= s * seq) & (rows < (s + 1) * seq), o, out)
    o_ref[...] = out.astype(BF16)


def _mem_attn_sample(q, mem_k, mem_v, cat, l, n_heads, row0, seq):
    dq = q.shape[1]
    n_seq, mlen = mem_k.shape[1], mem_k.shape[2]
    per = max(16 // seq, 1)
    tq = per * seq
    assert row0 % tq == 0 and n_seq % per == 0
    cb = cat.shape[1] // dq - 1
    kv = pl.BlockSpec((1, per, mlen, dq), lambda i: (l, i, 0, 0))
    return pl.pallas_call(
        functools.partial(_mem_sample_body, n_heads=n_heads, seq=seq),
        grid=(n_seq // per,),
        in_specs=[pl.BlockSpec((tq, dq), lambda i: (row0 // tq + i, 0)), kv, kv,
                  pl.BlockSpec(memory_space=pl.ANY)],
        out_specs=pl.BlockSpec((tq, dq), lambda i: (row0 // tq + i, cb)),
        out_shape=jax.ShapeDtypeStruct(cat.shape, cat.dtype),
        input_output_aliases={3: 0},
        compiler_params=_cparams(), name="mem_attn_sample",
    )(q, mem_k, mem_v, cat)


def _sink_attend(s, mask, sink, vb):
    s = jnp.where(mask, s, -jnp.inf)
    mx = jnp.maximum(jnp.max(s, axis=-1, keepdims=True), sink)
    p = jnp.exp(s - mx)
    den = jnp.sum(p, axis=-1, keepdims=True) + jnp.exp(sink - mx)
    return jnp.dot(p.astype(BF16), vb, preferred_element_type=F32) / den


def _win_prompt_body(sink_ref, q_ref, kvp_ref, kvc_ref, o_ref, *, n_kv, q_per_kv, dh, sink_row):
    b = pl.program_id(1)
    w = q_ref.shape[0]
    kvw = n_kv * dh
    qi = lax.broadcasted_iota(I32, (w, 2 * w), 0)
    kj = lax.broadcasted_iota(I32, (w, 2 * w), 1)
    mask = (kj > qi) & (kj <= qi + w) & ((kj >= w) | (b > 0))
    outs = []
    for h in range(n_kv):
        kc = slice(h * dh, (h + 1) * dh)
        vc = slice(kvw + h * dh, kvw + (h + 1) * dh)
        kb = jnp.concatenate([kvp_ref[:, kc], kvc_ref[:, kc]], axis=0).astype(BF16)
        vb = jnp.concatenate([kvp_ref[:, vc], kvc_ref[:, vc]], axis=0).astype(BF16)
        for g in range(q_per_kv):
            hq = h * q_per_kv + g
            s = lax.dot_general(q_ref[:, hq * dh:(hq + 1) * dh], kb, (((1,), (1,)), ((), ())),
                                preferred_element_type=F32) * dh ** -0.5
            outs.append(_sink_attend(s, mask, sink_ref[sink_row, hq], vb))
    o_ref[...] = jnp.concatenate(outs, axis=1).astype(BF16)


def _win_attn_prompt(q, kv, sinks, jl, n_batch, seq, n_kv, dh, d_out):
    dq = q.shape[1]
    q_per_kv = dq // (n_kv * dh)
    nb = seq // WINDOW
    kvw = kv.shape[1]
    grid_spec = pltpu.PrefetchScalarGridSpec(
        num_scalar_prefetch=1, grid=(n_batch, nb),
        in_specs=[pl.BlockSpec((WINDOW, dq), lambda n, b, s: (n * nb + b, 0)),
                  pl.BlockSpec((WINDOW, kvw), lambda n, b, s: (n * nb + jnp.maximum(b - 1, 0), 0)),
                  pl.BlockSpec((WINDOW, kvw), lambda n, b, s: (n * nb + b, 0))],
        out_specs=pl.BlockSpec((WINDOW, dq), lambda n, b, s: (n * nb + b, 0)))
    return pl.pallas_call(
        functools.partial(_win_prompt_body, n_kv=n_kv, q_per_kv=q_per_kv, dh=dh, sink_row=jl),
        grid_spec=grid_spec,
        out_shape=jax.ShapeDtypeStruct((q.shape[0], d_out), BF16),
        compiler_params=_cparams(2), name="win_attn_prompt",
    )(sinks, q, kv, kv)


def _win_sample_body(q_ref, k_ref, v_ref, sink_ref, o_ref, *, n_kv, seq, n_buf):
    rows = q_ref.shape[2]
    nk = k_ref.shape[2]
    assert seq & (seq - 1) == 0
    t = lax.broadcasted_iota(I32, (rows, nk), 0) & (seq - 1)
    kj = lax.broadcasted_iota(I32, (rows, nk), 1)
    diff = t - (kj - n_buf)
    mask = (diff >= 0) & (diff < WINDOW)
    dh = q_ref.shape[3]
    for h in range(n_kv):
        s = lax.dot_general(q_ref[0, h].astype(BF16), k_ref[0, h].astype(BF16),
                            (((1,), (1,)), ((), ())), preferred_element_type=F32) * dh ** -0.5
        o_ref[0, h] = _sink_attend(s, mask, sink_ref[0, h], v_ref[0, h].astype(BF16))


def _win_attn_sample(q, kk, vv, sink_rows, seq, n_buf):
    n, n_kv, rows, dh = q.shape
    nk = kk.shape[2]
    blk = lambda r: pl.BlockSpec((1, n_kv, r, dh), lambda i: (i, 0, 0, 0))
    return pl.pallas_call(
        functools.partial(_win_sample_body, n_kv=n_kv, seq=seq, n_buf=n_buf),
        grid=(n,),
        in_specs=[blk(rows), blk(nk), blk(nk),
                  pl.BlockSpec((1, n_kv, rows, 1), lambda i: (0, 0, 0, 0))],
        out_specs=blk(rows),
        out_shape=jax.ShapeDtypeStruct((n, n_kv, rows, dh), F32),
        compiler_params=_cparams(), name="win_attn_sample",
    )(q, kk, vv, sink_rows)


def kernel(x_prompt, x_sample, mem_prompt, cache_mem_k, cache_mem_v, cache_win_k, cache_win_v, w_in_a, sgu_g, sgu_b, w_spatial, b_spatial, w_in_b, w_kv, sinks, w_mem_kv, w_out, ln_g, ln_b, router_w, router_b, w_gate_up, w_down, shared_gate_up, shared_down):
    nb, seq, d = x_prompt.shape
    ns, sseq, _ = x_sample.shape
    depth = w_out.shape[0]
    n_a = w_in_a.shape[0]
    mem_len = mem_prompt.shape[1]
    mem_heads, mem_dh = cache_mem_k.shape[3], cache_mem_k.shape[4]
    d_mem = mem_heads * mem_dh
    d_tok = d - d_mem
    n_buf, n_kv, dh = cache_win_k.shape[1], cache_win_k.shape[2], cache_win_k.shape[3]
    kvw = n_kv * dh
    q_per_kv = d_tok // kvw
    chunk = w_spatial.shape[2]
    n_groups = w_spatial.shape[1]
    alpha = float((2 * depth) ** 0.25)
    mp, ms = nb * seq, ns * sseq
    m = mp + ms
    p = dict(router_w=router_w, router_b=router_b, w_gate_up=w_gate_up, w_down=w_down,
             shared_gate_up=shared_gate_up, shared_down=shared_down)

    memb = mem_prompt.reshape(nb * mem_len, d).astype(BF16)
    tm_mem = _pick(nb * mem_len, (512, 256, 128))
    tn_mem = _pick(d_mem, (512, 256, 128))
    n_j, n_r = d_mem // tn_mem, nb * mem_len // tm_mem
    lay = np.repeat(np.arange(depth), n_j * n_r)
    jj = np.tile(np.repeat(np.arange(n_j), n_r), depth)
    rr = np.tile(np.arange(n_r), depth * n_j)
    mem_tables = tuple(jnp.asarray(c, I32) for c in
                       (lay, jj, rr, lay, (rr == 0).astype(np.int64), np.ones_like(lay)))
    mem_kw = dict(tm=tm_mem, tn=tn_mem, n_cols=d_mem, out_dtype=F32, n_out=depth)
    pmem_k = _gmm(mem_tables, memb, w_mem_kv, name="mem_k_proj", col0=0, **mem_kw)
    pmem_v = _gmm(mem_tables, memb, w_mem_kv, name="mem_v_proj", col0=d_mem, **mem_kw)
    pmem_k = pmem_k.reshape(depth, nb, mem_len, d_mem)
    pmem_v = pmem_v.reshape(depth, nb, mem_len, d_mem)
    smem_k = cache_mem_k.reshape(depth, ns, mem_len, d_mem)
    smem_v = cache_mem_v.reshape(depth, ns, mem_len, d_mem)

    xf = jnp.concatenate([x_prompt.reshape(mp, d), x_sample.reshape(ms, d)], axis=0)
    xb = xf.astype(BF16)

    tril = jnp.tril(jnp.ones((chunk, chunk), F32))
    c_s = min(sseq, chunk)
    reps = chunk // c_s
    eye = jnp.eye(reps, dtype=F32)
    w_p = w_spatial * tril
    w_c = (w_spatial[:, :, :c_s, :c_s] * tril[:c_s, :c_s])
    w_s = jnp.einsum("ab,lgts->lgatbs", eye, w_c).reshape(n_a, n_groups, chunk, chunk)
    wmix = jnp.stack([w_p, w_s], axis=1)
    bias_p = jnp.swapaxes(b_spatial, 1, 2)
    bias_s = jnp.tile(bias_p[:, :c_s], (1, reps, 1))
    bias = jnp.stack([bias_p, bias_s], axis=1)

    chunk_v = []
    kv = k_new = v_new = kk = vv = None
    for l in range(depth):
        if l < n_a:
            hu = _dense(xb, w_in_a, l, name="in_a_u", n_cols=d_tok, col0=0, act="gelu", out_dtype=BF16)
            hv = _dense(xb, w_in_a, l, name="in_a_v", n_cols=d_tok, col0=d_tok, act="gelu", out_dtype=F32)
            qm = _dense(xb, w_in_a, l, name="in_a_qmem", n_cols=d_mem, col0=2 * d_tok, out_dtype=BF16)
            cat, v_s = _sgu(hu, hv, sgu_g[l], sgu_b[l], wmix[l], bias[l], mp // chunk, d)
            chunk_v.append(v_s.reshape(ns, sseq, d_tok))
        else:
            jl = l - n_a
            if kv is None:
                kv = _dense(xb, w_kv.reshape(1, d, 2 * kvw), 0, name="kv_proj", n_cols=2 * kvw,
                            out_dtype=F32, tn=_pick(2 * kvw, (384, 256, 128)))
                k_new = kv[:, :kvw]
                v_new = kv[:, kvw:]
                kk = jnp.concatenate([cache_win_k, k_new[mp:].reshape(ns, sseq, n_kv, dh)], axis=1)
                vv = jnp.concatenate([cache_win_v, v_new[mp:].reshape(ns, sseq, n_kv, dh)], axis=1)
            q = _dense(xb, w_in_b, jl, name="in_b_q", n_cols=d_tok, col0=0, out_dtype=BF16)
            qm = _dense(xb, w_in_b, jl, name="in_b_qmem", n_cols=d_mem, col0=d_tok, out_dtype=BF16)
            cat = _win_attn_prompt(q, kv, sinks, jl, nb, seq, n_kv, dh, d)
            q_s = q[mp:].astype(F32).reshape(ns, sseq, n_kv, q_per_kv, dh)
            q_s = q_s.transpose(0, 2, 3, 1, 4).reshape(ns, n_kv, q_per_kv * sseq, dh)
            sink_rows = jnp.repeat(sinks[jl].reshape(1, n_kv, q_per_kv), sseq, axis=2)[..., None]
            o_s = _win_attn_sample(q_s, kk.transpose(0, 2, 1, 3), vv.transpose(0, 2, 1, 3),
                                   sink_rows, sseq, n_buf)
            tok_s = o_s.reshape(ns, n_kv, q_per_kv, sseq, dh).transpose(0, 3, 1, 2, 4)
            cat = lax.dynamic_update_slice(cat, tok_s.reshape(ms, d_tok).astype(BF16), (mp, 0))
        cat = _mem_attn_prompt(qm, pmem_k, pmem_v, cat, l, mem_heads, nb, seq)
        cat = _mem_attn_sample(qm, smem_k, smem_v, cat, l, mem_heads, mp, sseq)
        mix = _dense(cat, w_out, l, name="out_proj", n_cols=d, out_dtype=F32)
        xf, xb, x_slabs = _resid_ln(xf, mix, ln_g[l, 0], ln_b[l, 0], alpha)
        xf, xb = _moe_ffn(xf, xb, x_slabs, l, p, ln_g[l, 1], ln_b[l, 1], alpha)

    y_prompt = xf[:mp].reshape(nb, seq, d)
    y_sample = xf[mp:].reshape(ns, sseq, d)
    mshape = (depth, nb, mem_len, mem_heads, mem_dh)
    keep_p = min(WINDOW, seq)
    k_p = k_new[:mp].reshape(nb, seq, n_kv, dh)[:, -keep_p:]
    v_p = v_new[:mp].reshape(nb, seq, n_kv, dh)[:, -keep_p:]
    return (y_prompt, y_sample, pmem_k.reshape(mshape), pmem_v.reshape(mshape), k_p, v_p,
            kk[:, -n_buf:], vv[:, -n_buf:], jnp.stack(chunk_v))
```

```python
import functools

import numpy as np
import jax
import jax.numpy as jnp
from jax import lax
from jax.experimental import pallas as pl
from jax.experimental.pallas import tpu as pltpu

F32, BF16, I32, U32 = jnp.float32, jnp.bfloat16, jnp.int32, jnp.uint32

LANES = 128
TOP_K = 8
N_GROUPS = 8
TOPK_GROUPS = 4
ROUTED_SCALE = 2.5
WINDOW = 128
LN_EPS = 1e-5
MOE_BM = 256
VMEM_LIMIT = 56 * 1024 * 1024


def _pick(n, prefs):
    for p in prefs:
        if n % p == 0:
            return p
    return n


def _cparams(n_axes=1):
    return pltpu.CompilerParams(dimension_semantics=("arbitrary",) * n_axes,
                                vmem_limit_bytes=VMEM_LIMIT)


def _layer_norm(xf, g, b):
    mu = jnp.mean(xf, axis=-1, keepdims=True)
    var = jnp.mean(jnp.square(xf - mu), axis=-1, keepdims=True)
    return (xf - mu) * lax.rsqrt(var + LN_EPS) * g + b


def _pack_rows(a):
    half = a.shape[1] // 2
    lo = pltpu.bitcast(a[:, :half].astype(BF16).astype(F32), U32)
    hi = pltpu.bitcast(a[:, half:].astype(BF16).astype(F32), U32)
    return (lo >> 16) | (hi & jnp.uint32(0xFFFF0000))


def _store_slabs(o_ref, u):
    t = u.shape[0]
    s = u.shape[1] // LANES
    for c in range(s):
        o_ref[pl.ds(c, t, stride=s), :] = u[:, c * LANES:(c + 1) * LANES]


def _unpack_words(u):
    lo = pltpu.bitcast(u << 16, F32)
    hi = pltpu.bitcast(u & jnp.uint32(0xFFFF0000), F32)
    return lo, hi


def _cast_block(src_ref, dst_ref, col0):
    k, tn = src_ref.shape[1], src_ref.shape[2]
    ck = _pick(k, (512, 256, 128))

    def body(c, carry):
        rows = pl.ds(pl.multiple_of(c * ck, ck), ck)
        dst_ref[rows, col0:col0 + tn] = src_ref[0, rows, :].astype(BF16)
        return carry

    lax.fori_loop(0, k // ck, body, 0)


def _gmm_body(e_ref, j_ref, r_ref, eo_ref, first_ref, valid_ref, x_ref, *rest, gated, act, slabs):
    del e_ref, j_ref, r_ref, eo_ref
    if gated:
        w_ref, w2_ref, o_ref, wb_ref = rest
    else:
        w_ref, o_ref, wb_ref = rest
    tn = w_ref.shape[2]
    i = pl.program_id(0)

    @pl.when(first_ref[i] == 1)
    def _cast():
        _cast_block(w_ref, wb_ref, 0)
        if gated:
            _cast_block(w2_ref, wb_ref, tn)

    @pl.when(valid_ref[i] == 1)
    def _compute():
        a = jnp.dot(x_ref[...], wb_ref[...], preferred_element_type=F32)
        if gated:
            a = jax.nn.silu(a[:, :tn]) * a[:, tn:]
        elif act == "gelu":
            a = jax.nn.gelu(a)
        if slabs:
            _store_slabs(o_ref.at[0], _pack_rows(a))
        else:
            o_ref[0] = a.astype(o_ref.dtype)


def _gmm(tables, x, w, *, name, tm, tn, n_cols, col0=0, col0_up=None, act=None, out_dtype=F32,
         n_out=1, e0=0, slabs=False):
    m, k = x.shape
    gated = col0_up is not None
    n_items = tables[0].shape[0]
    assert m % tm == 0 and n_cols % tn == 0 and col0 % tn == 0
    if slabs:
        assert tn == n_cols and n_cols % (2 * LANES) == 0
        spr = n_cols // (2 * LANES)
        out_spec = pl.BlockSpec((1, tm * spr, LANES), lambda i, e, j, r, eo, f, v: (eo[i], r[i], 0))
        out_shape = jax.ShapeDtypeStruct((n_out, m * spr, LANES), U32)
    else:
        out_spec = pl.BlockSpec((1, tm, tn), lambda i, e, j, r, eo, f, v: (eo[i], r[i], j[i]))
        out_shape = jax.ShapeDtypeStruct((n_out, m, n_cols), out_dtype)
    jb0 = col0 // tn
    in_specs = [
        pl.BlockSpec((tm, k), lambda i, e, j, r, eo, f, v: (r[i], 0)),
        pl.BlockSpec((1, k, tn), lambda i, e, j, r, eo, f, v: (e0 + e[i], 0, jb0 + j[i])),
    ]
    operands = [x, w]
    if gated:
        assert col0_up % tn == 0 and tn % LANES == 0
        jb1 = col0_up // tn
        in_specs.append(pl.BlockSpec((1, k, tn), lambda i, e, j, r, eo, f, v: (e0 + e[i], 0, jb1 + j[i])))
        operands.append(w)
    scratch = [pltpu.VMEM((k, 2 * tn if gated else tn), BF16)]
    grid_spec = pltpu.PrefetchScalarGridSpec(
        num_scalar_prefetch=6, grid=(n_items,), in_specs=in_specs, out_specs=out_spec,
        scratch_shapes=scratch)
    return pl.pallas_call(
        functools.partial(_gmm_body, gated=gated, act=act, slabs=slabs),
        grid_spec=grid_spec, out_shape=out_shape, compiler_params=_cparams(), name=name,
    )(*tables, *operands)


def _dense_tables(n_j, n_r, e=0):
    j = np.repeat(np.arange(n_j), n_r)
    r = np.tile(np.arange(n_r), n_j)
    z = np.zeros_like(j)
    cols = (z + e, j, r, z, (r == 0).astype(np.int64), z + 1)
    return tuple(jnp.asarray(c, I32) for c in cols)


def _dense(x, w, e, *, name, n_cols, col0=0, col0_up=None, act=None, out_dtype=F32, tm=None, tn=None):
    m = x.shape[0]
    tm = tm or _pick(m, (768, 512, 256, 128))
    tn = tn or _pick(n_cols, (512, 384, 256, 128))
    tables = _dense_tables(n_cols // tn, m // tm, e)
    return _gmm(tables, x, w, name=name, tm=tm, tn=tn, n_cols=n_cols, col0=col0, col0_up=col0_up,
                act=act, out_dtype=out_dtype)[0]


def _resid_ln_body(x_ref, y_ref, g_ref, b_ref, o_ref, ob_ref, os_ref, *, alpha):
    o = _layer_norm(alpha * x_ref[...] + y_ref[...], g_ref[...], b_ref[...])
    o_ref[...] = o
    ob_ref[...] = o.astype(BF16)
    _store_slabs(os_ref, _pack_rows(o))


def _resid_ln(x, y, g, b, alpha):
    m, d = x.shape
    tm = _pick(m, (256, 128))
    spr = d // (2 * LANES)
    row = pl.BlockSpec((tm, d), lambda i: (i, 0))
    vec = pl.BlockSpec((1, d), lambda i: (0, 0))
    return pl.pallas_call(
        functools.partial(_resid_ln_body, alpha=alpha),
        grid=(m // tm,), in_specs=[row, row, vec, vec],
        out_specs=[row, row, pl.BlockSpec((tm * spr, LANES), lambda i: (i, 0))],
        out_shape=[jax.ShapeDtypeStruct((m, d), F32), jax.ShapeDtypeStruct((m, d), BF16),
                   jax.ShapeDtypeStruct((m * spr, LANES), U32)],
        compiler_params=_cparams(), name="resid_ln",
    )(x, y, g.reshape(1, d), b.reshape(1, d))


def _router_body(x_ref, rw_ref, rb_ref, eid_ref, pos_ref, gate_ref, cnt_ref, base_ref, *, n_exp):
    tm = x_ref.shape[0]
    gsz = n_exp // N_GROUPS
    i = pl.program_id(0)

    @pl.when(i == 0)
    def _init():
        base_ref[...] = jnp.zeros_like(base_ref)

    logits = jnp.dot(x_ref[...], rw_ref[...], precision=lax.Precision.HIGHEST,
                     preferred_element_type=F32)
    s = jax.nn.sigmoid(logits.T[:n_exp])
    sb = s + rb_ref[...]
    row = lax.broadcasted_iota(I32, (n_exp, tm), 0)
    assert gsz & (gsz - 1) == 0
    grp = lax.shift_right_logical(row, gsz.bit_length() - 1)
    bcast = lambda a, e2: jnp.broadcast_to(a[e2:e2 + 1, :], (n_exp, tm))

    def count_beaten(a, idx, same=None):
        n = jnp.zeros((n_exp, tm), I32)
        for e2 in range(n_exp):
            c = bcast(a, e2)
            beats = (c > a) | ((c == a) & (e2 < idx))
            if same is not None:
                beats = beats & (same == e2 // gsz)
            n = n + jnp.where(beats, 1, 0)
        return n

    top2 = jnp.where(count_beaten(sb, row, grp) < 2, sb, 0.0)
    gsum = jnp.zeros((n_exp, tm), F32)
    for e2 in range(n_exp):
        gsum = gsum + jnp.where(grp == e2 // gsz, bcast(top2, e2), 0.0)
    n_beat = jnp.zeros((n_exp, tm), I32)
    for g2 in range(N_GROUPS):
        c = bcast(gsum, g2 * gsz)
        n_beat = n_beat + jnp.where((c > gsum) | ((c == gsum) & (g2 < grp)), 1, 0)
    v = jnp.where(n_beat < TOPK_GROUPS, sb, -jnp.inf)
    sel = count_beaten(v, row) < TOP_K
    w = jnp.where(sel, s, 0.0)
    gate = w / jnp.sum(w, axis=0, keepdims=True) * ROUTED_SCALE

    sel_b = jnp.where(sel, 1.0, 0.0).astype(BF16)
    ti = lax.broadcasted_iota(I32, (tm, tm), 0)
    tj = lax.broadcasted_iota(I32, (tm, tm), 1)
    earlier = jnp.where(ti < tj, 1.0, 0.0).astype(BF16)
    pos = base_ref[:, 0:1] + jnp.dot(sel_b, earlier, preferred_element_type=F32)
    base_ref[...] = base_ref[...] + jnp.sum(sel_b.astype(F32), axis=1, keepdims=True)
    cnt_ref[...] = base_ref[...].astype(I32)

    ei = lax.broadcasted_iota(I32, (n_exp, n_exp), 0)
    ej = lax.broadcasted_iota(I32, (n_exp, n_exp), 1)
    upto = jnp.where(ej <= ei, 1.0, 0.0).astype(BF16)
    slot = jnp.dot(upto, sel_b, preferred_element_type=F32)
    row_f = row.astype(F32)
    for k in range(TOP_K):
        m = sel & (slot == float(k + 1))
        eid_ref[k:k + 1, :] = jnp.sum(jnp.where(m, row_f, 0.0), axis=0, keepdims=True).astype(I32)
        pos_ref[k:k + 1, :] = jnp.sum(jnp.where(m, pos, 0.0), axis=0, keepdims=True).astype(I32)
        gate_ref[k:k + 1, :] = jnp.sum(jnp.where(m, gate, 0.0), axis=0, keepdims=True)


def _router(x, rw, rb):
    m, d = x.shape
    n_exp = rw.shape[1]
    n_pad = -(-n_exp // LANES) * LANES
    tm = _pick(m, (256, 128))
    slot = pl.BlockSpec((TOP_K, tm), lambda i: (0, i))
    eid, pos, gate, cnt = pl.pallas_call(
        functools.partial(_router_body, n_exp=n_exp),
        grid=(m // tm,),
        in_specs=[pl.BlockSpec((tm, d), lambda i: (i, 0)),
                  pl.BlockSpec((d, n_pad), lambda i: (0, 0)),
                  pl.BlockSpec((n_exp, 1), lambda i: (0, 0))],
        out_specs=[slot, slot, slot, pl.BlockSpec((n_exp, LANES), lambda i: (0, 0))],
        out_shape=[jax.ShapeDtypeStruct((TOP_K, m), I32), jax.ShapeDtypeStruct((TOP_K, m), I32),
                   jax.ShapeDtypeStruct((TOP_K, m), F32), jax.ShapeDtypeStruct((n_exp, LANES), I32)],
        scratch_shapes=[pltpu.VMEM((n_exp, LANES), F32)],
        compiler_params=_cparams(), name="router",
    )(x, jnp.pad(rw, ((0, 0), (0, n_pad - n_exp))), rb.reshape(n_exp, 1))
    return eid.T, pos.T, gate.T, cnt[:, 0]


def _gather_body(nblk_ref, tok_ref, nxt_ref, x_hbm, o_ref, buf, sem):
    i = pl.program_id(0)
    nblk = nblk_ref[0]
    spr = x_hbm.shape[1]
    bm = buf.shape[0] // (2 * spr)
    slot = lax.rem(i, 2)

    def row_copy(ids_ref, s, r):
        dst = buf.at[pl.ds(pl.multiple_of((s * bm + r) * spr, spr), spr), :]
        return pltpu.make_async_copy(x_hbm.at[ids_ref[0, 0, r]], dst, sem.at[s])

    def start_all(ids_ref, s):
        def body(r, c):
            row_copy(ids_ref, s, r).start()
            return c

        lax.fori_loop(0, bm, body, 0, unroll=8)

    @pl.when(i == 0)
    def _first():
        start_all(tok_ref, 0)

    @pl.when(i + 1 < nblk)
    def _prefetch():
        start_all(nxt_ref, 1 - slot)

    @pl.when(i < nblk)
    def _block():
        def wait(r, c):
            row_copy(tok_ref, slot, r).wait()
            return c

        lax.fori_loop(0, bm, wait, 0, unroll=8)
        base = slot * (bm * spr)
        los, his = [], []
        for c in range(spr):
            lo, hi = _unpack_words(buf[pl.ds(base + c, bm, stride=spr), :])
            los.append(lo.astype(BF16))
            his.append(hi.astype(BF16))
        o_ref[...] = jnp.concatenate(los + his, axis=1)


def _gather_rows(x_slabs, row_tok, n_blocks, bm):
    nb_max = row_tok.shape[0] // bm
    _, spr, _ = x_slabs.shape
    d = spr * 2 * LANES
    ids = row_tok.reshape(nb_max, 1, bm)
    grid_spec = pltpu.PrefetchScalarGridSpec(
        num_scalar_prefetch=1, grid=(nb_max,),
        in_specs=[pl.BlockSpec((1, 1, bm), lambda i, nb: (jnp.minimum(i, nb[0] - 1), 0, 0),
                               memory_space=pltpu.SMEM),
                  pl.BlockSpec((1, 1, bm), lambda i, nb: (jnp.minimum(i + 1, nb[0] - 1), 0, 0),
                               memory_space=pltpu.SMEM),
                  pl.BlockSpec(memory_space=pl.ANY)],
        out_specs=pl.BlockSpec((bm, d), lambda i, nb: (jnp.minimum(i, nb[0] - 1), 0)),
        scratch_shapes=[pltpu.VMEM((2 * bm * spr, LANES), U32), pltpu.SemaphoreType.DMA((2,))])
    return pl.pallas_call(
        _gather_body, grid_spec=grid_spec,
        out_shape=jax.ShapeDtypeStruct((nb_max * bm, d), BF16),
        compiler_params=_cparams(), name="moe_gather",
    )(n_blocks.reshape(1), ids, ids, x_slabs)


def _combine_body(dest_ref, nxt_ref, gate_ref, x_ref, sh_ref, g_ref, b_ref, y_hbm, o_ref, ob_ref,
                  buf, sem, *, alpha):
    i = pl.program_id(0)
    tt = x_ref.shape[0]
    spr = y_hbm.shape[1]
    slot = lax.rem(i, 2)

    def row_copy(ids_ref, s, t, k):
        off = pl.multiple_of(((s * TOP_K + k) * tt + t) * spr, spr)
        return pltpu.make_async_copy(y_hbm.at[ids_ref[0, 0, t * TOP_K + k]],
                                     buf.at[pl.ds(off, spr), :], sem.at[s])

    def start_all(ids_ref, s):
        def body(t, c):
            for k in range(TOP_K):
                row_copy(ids_ref, s, t, k).start()
            return c

        lax.fori_loop(0, tt, body, 0)

    @pl.when(i == 0)
    def _first():
        start_all(dest_ref, 0)

    @pl.when(i + 1 < pl.num_programs(0))
    def _prefetch():
        start_all(nxt_ref, 1 - slot)

    acc = alpha * x_ref[...] + sh_ref[...]

    def wait(t, c):
        for k in range(TOP_K):
            row_copy(dest_ref, slot, t, k).wait()
        return c

    lax.fori_loop(0, tt, wait, 0)
    los, his = [None] * spr, [None] * spr
    for k in range(TOP_K):
        gk = gate_ref[:, k:k + 1]
        base = (slot * TOP_K + k) * (tt * spr)
        for c in range(spr):
            lo, hi = _unpack_words(buf[pl.ds(base + c, tt, stride=spr), :])
            los[c] = gk * lo if k == 0 else los[c] + gk * lo
            his[c] = gk * hi if k == 0 else his[c] + gk * hi
    routed = jnp.concatenate(los + his, axis=1)
    o = _layer_norm(acc + routed, g_ref[...], b_ref[...])
    o_ref[...] = o
    ob_ref[...] = o.astype(BF16)


def _combine_ln(dest, gate, x, shared, y_slabs, g, b, alpha):
    m, d = x.shape
    spr = y_slabs.shape[1]
    tt = _pick(m, (64, 32, 16, 8))
    n = m // tt
    row = pl.BlockSpec((tt, d), lambda i: (i, 0))
    vec = pl.BlockSpec((1, d), lambda i: (0, 0))
    ids = dest.reshape(n, 1, tt * TOP_K)
    return pl.pallas_call(
        functools.partial(_combine_body, alpha=alpha),
        grid=(n,),
        in_specs=[pl.BlockSpec((1, 1, tt * TOP_K), lambda i: (i, 0, 0), memory_space=pltpu.SMEM),
                  pl.BlockSpec((1, 1, tt * TOP_K), lambda i: (jnp.minimum(i + 1, n - 1), 0, 0),
                               memory_space=pltpu.SMEM),
                  pl.BlockSpec((tt, TOP_K), lambda i: (i, 0)), row, row, vec, vec,
                  pl.BlockSpec(memory_space=pl.ANY)],
        out_specs=[row, row],
        out_shape=[jax.ShapeDtypeStruct((m, d), F32), jax.ShapeDtypeStruct((m, d), BF16)],
        scratch_shapes=[pltpu.VMEM((2 * TOP_K * tt * spr, LANES), U32), pltpu.SemaphoreType.DMA((2,))],
        compiler_params=_cparams(), name="moe_combine",
    )(ids, ids, gate, x, shared, g.reshape(1, d), b.reshape(1, d), y_slabs)


def _moe_tables(counts, n_j, nb_max, bm):
    n_exp = counts.shape[0]
    nb = (counts + bm - 1) // bm
    blk_end = jnp.cumsum(nb)
    blk_start = blk_end - nb
    total = blk_end[-1]
    i = jnp.arange(n_j * nb_max, dtype=I32)
    valid = i < n_j * total
    ids = jnp.arange(n_exp, dtype=I32)
    own = (i[:, None] >= n_j * blk_start[None, :]) & (i[:, None] < n_j * blk_end[None, :])
    pick = lambda a: jnp.sum(jnp.where(own, a[None, :], 0), axis=1)
    e, bs, nbe = pick(ids), pick(blk_start), jnp.maximum(pick(nb), 1)
    local = i - n_j * bs
    j = local // nbe
    rb = local - j * nbe
    r = bs + rb
    e_last = jnp.max(jnp.where(nb > 0, ids, 0))
    e = jnp.where(valid, e, e_last)
    j = jnp.where(valid, j, n_j - 1)
    r = jnp.where(valid, r, total - 1)
    first = ((rb == 0) & valid).astype(I32)
    tables = (e, j, r, jnp.zeros_like(i), first, valid)
    return tuple(t.astype(I32) for t in tables), blk_start, total


def _moe_ffn(xf, xb, x_slabs, l, p, ln_g, ln_b, alpha):
    m, d = xf.shape
    spr = d // (2 * LANES)
    n_exp = p["router_w"].shape[-1]
    f = p["w_down"].shape[2]
    bm = MOE_BM
    nb_max = (m * TOP_K) // bm + n_exp
    eid, pos, gate, counts = _router(xf, p["router_w"][l], p["router_b"][l])
    cn = _pick(f, (384, 256, 128))
    up_tables, blk_start, total = _moe_tables(counts, f // cn, nb_max, bm)
    dn_tables, _, _ = _moe_tables(counts, 1, nb_max, bm)
    start_of = jnp.sum(jnp.where(eid[..., None] == jnp.arange(n_exp, dtype=I32), blk_start, 0), axis=-1)
    dest = bm * start_of + pos
    tok = jnp.broadcast_to(jnp.arange(m, dtype=I32)[:, None], (m, TOP_K))
    row_tok = jnp.zeros((nb_max * bm,), I32).at[dest.reshape(-1)].set(
        tok.reshape(-1), unique_indices=True)
    per = 2 if nb_max % 2 == 0 else 1
    xs = _gather_rows(x_slabs.reshape(m, spr, LANES), row_tok,
                      ((total + per - 1) // per).astype(I32), per * bm)
    w_gu = p["w_gate_up"].reshape(-1, d, 2 * f)
    w_dn = p["w_down"].reshape(-1, f, d)
    h = _gmm(up_tables, xs, w_gu, name="moe_up", tm=bm, tn=cn, n_cols=f, col0=0, col0_up=f,
             out_dtype=BF16, e0=l * n_exp)[0]
    y = _gmm(dn_tables, h, w_dn, name="moe_down", tm=bm, tn=d, n_cols=d, e0=l * n_exp, slabs=True)[0]
    y = y.reshape(nb_max * bm, spr, LANES)
    hs = _dense(xb, p["shared_gate_up"], l, name="shared_up", n_cols=f, col0=0, col0_up=f,
                out_dtype=BF16, tn=cn)
    ys = _dense(hs, p["shared_down"], l, name="shared_down", n_cols=d, out_dtype=F32,
                tn=_pick(d, (1024, 512, 256, 128)))
    return _combine_ln(dest, gate, xf, ys, y, ln_g, ln_b, alpha)


def _sgu_body(hu_ref, hv_ref, g_ref, b_ref, wm_ref, bias_ref, o_ref, v_ref, *, n_groups):
    v = _layer_norm(hv_ref[...], g_ref[...], b_ref[...])
    v_ref[...] = v
    vb = v.astype(BF16)
    gd = v.shape[1] // n_groups
    for g in range(n_groups):
        cols = slice(g * gd, (g + 1) * gd)
        mixed = jnp.dot(wm_ref[0, g].astype(BF16), vb[:, cols], preferred_element_type=F32)
        mixed = mixed + bias_ref[0][:, g:g + 1]
        o_ref[:, cols] = (hu_ref[:, cols].astype(F32) * mixed).astype(BF16)


def _sgu(hu, hv, g, b, wmix, bias, n_prompt_blocks, d_out):
    m, dt = hv.shape
    c = wmix.shape[-1]
    ng = wmix.shape[1]
    nblk = m // c
    npb = n_prompt_blocks
    row = pl.BlockSpec((c, dt), lambda i: (i, 0))
    vec = pl.BlockSpec((1, dt), lambda i: (0, 0))
    kind = lambda i: jnp.where(i >= npb, 1, 0)
    return pl.pallas_call(
        functools.partial(_sgu_body, n_groups=ng),
        grid=(nblk,),
        in_specs=[row, row, vec, vec,
                  pl.BlockSpec((1, ng, c, c), lambda i: (kind(i), 0, 0, 0)),
                  pl.BlockSpec((1, c, ng), lambda i: (kind(i), 0, 0))],
        out_specs=[row, pl.BlockSpec((c, dt), lambda i: (jnp.maximum(i - npb, 0), 0))],
        out_shape=[jax.ShapeDtypeStruct((m, d_out), BF16),
                   jax.ShapeDtypeStruct(((nblk - npb) * c, dt), F32)],
        compiler_params=_cparams(), name="spatial_gate",
    )(hu, hv, g.reshape(1, dt), b.reshape(1, dt), wmix, bias)


def _softmax_rows(s):
    mx = jnp.max(s, axis=-1, keepdims=True)
    p = jnp.exp(s - mx)
    return p / jnp.sum(p, axis=-1, keepdims=True)


def _mem_heads(q, k_ref, v_ref, n_heads):
    dh = q.shape[1] // n_heads
    outs = []
    for hh in range(n_heads):
        cols = slice(hh * dh, (hh + 1) * dh)
        s = lax.dot_general(q[:, cols], k_ref[:, cols].astype(BF16), (((1,), (1,)), ((), ())),
                            preferred_element_type=F32) * dh ** -0.5
        p = _softmax_rows(s).astype(BF16)
        outs.append(jnp.dot(p, v_ref[:, cols].astype(BF16), preferred_element_type=F32))
    return jnp.concatenate(outs, axis=1)


def _mem_prompt_body(q_ref, k_ref, v_ref, cat_ref, o_ref, *, n_heads):
    del cat_ref
    o_ref[...] = _mem_heads(q_ref[...], k_ref.at[0, 0], v_ref.at[0, 0], n_heads).astype(BF16)


def _mem_attn_prompt(q, mem_k, mem_v, cat, l, n_heads, n_batch, seq):
    dq = q.shape[1]
    mlen = mem_k.shape[2]
    tq = _pick(seq, (512, 256, 128))
    nt = seq // tq
    cb = cat.shape[1] // dq - 1
    kv = pl.BlockSpec((1, 1, mlen, dq), lambda n, i: (l, n, 0, 0))
    return pl.pallas_call(
        functools.partial(_mem_prompt_body, n_heads=n_heads),
        grid=(n_batch, nt),
        in_specs=[pl.BlockSpec((tq, dq), lambda n, i: (n * nt + i, 0)), kv, kv,
                  pl.BlockSpec(memory_space=pl.ANY)],
        out_specs=pl.BlockSpec((tq, dq), lambda n, i: (n * nt + i, cb)),
        out_shape=jax.ShapeDtypeStruct(cat.shape, cat.dtype),
        input_output_aliases={3: 0},
        compiler_params=_cparams(2), name="mem_attn_prompt",
    )(q, mem_k, mem_v, cat)


def _mem_sample_body(q_ref, k_ref, v_ref, cat_ref, o_ref, *, n_heads, seq):
    del cat_ref
    q = q_ref[...]
    rows = lax.broadcasted_iota(I32, (q.shape[0], 1), 0)
    out = jnp.zeros(q.shape, F32)
    for s in range(q.shape[0] // seq):
        o = _mem_heads(q, k_ref.at[0, s], v_ref.at[0, s], n_heads)
        out = jnp.where((rows >= s * seq) & (rows < (s + 1) * seq), o, out)
    o_ref[...] = out.astype(BF16)


def _mem_attn_sample(q, mem_k, mem_v, cat, l, n_heads, row0, seq):
    dq = q.shape[1]
    n_seq, mlen = mem_k.shape[1], mem_k.shape[2]
    per = max(16 // seq, 1)
    tq = per * seq
    assert row0 % tq == 0 and n_seq % per == 0
    cb = cat.shape[1] // dq - 1
    kv = pl.BlockSpec((1, per, mlen, dq), lambda i: (l, i, 0, 0))
    return pl.pallas_call(
        functools.partial(_mem_sample_body, n_heads=n_heads, seq=seq),
        grid=(n_seq // per,),
        in_specs=[pl.BlockSpec((tq, dq), lambda i: (row0 // tq + i, 0)), kv, kv,
                  pl.BlockSpec(memory_space=pl.ANY)],
        out_specs=pl.BlockSpec((tq, dq), lambda i: (row0 // tq + i, cb)),
        out_shape=jax.ShapeDtypeStruct(cat.shape, cat.dtype),
        input_output_aliases={3: 0},
        compiler_params=_cparams(), name="mem_attn_sample",
    )(q, mem_k, mem_v, cat)


def _sink_attend(s, mask, sink, vb):
    s = jnp.where(mask, s, -jnp.inf)
    mx = jnp.maximum(jnp.max(s, axis=-1, keepdims=True), sink)
    p = jnp.exp(s - mx)
    den = jnp.sum(p, axis=-1, keepdims=True) + jnp.exp(sink - mx)
    return jnp.dot(p.astype(BF16), vb, preferred_element_type=F32) / den


def _win_prompt_body(sink_ref, q_ref, kvp_ref, kvc_ref, o_ref, *, n_kv, q_per_kv, dh, sink_row):
    b = pl.program_id(1)
    w = q_ref.shape[0]
    kvw = n_kv * dh
    qi = lax.broadcasted_iota(I32, (w, 2 * w), 0)
    kj = lax.broadcasted_iota(I32, (w, 2 * w), 1)
    mask = (kj > qi) & (kj <= qi + w) & ((kj >= w) | (b > 0))
    outs = []
    for h in range(n_kv):
        kc = slice(h * dh, (h + 1) * dh)
        vc = slice(kvw + h * dh, kvw + (h + 1) * dh)
        kb = jnp.concatenate([kvp_ref[:, kc], kvc_ref[:, kc]], axis=0).astype(BF16)
        vb = jnp.concatenate([kvp_ref[:, vc], kvc_ref[:, vc]], axis=0).astype(BF16)
        for g in range(q_per_kv):
            hq = h * q_per_kv + g
            s = lax.dot_general(q_ref[:, hq * dh:(hq + 1) * dh], kb, (((1,), (1,)), ((), ())),
                                preferred_element_type=F32) * dh ** -0.5
            outs.append(_sink_attend(s, mask, sink_ref[sink_row, hq], vb))
    o_ref[...] = jnp.concatenate(outs, axis=1).astype(BF16)


def _win_attn_prompt(q, kv, sinks, jl, n_batch, seq, n_kv, dh, d_out):
    dq = q.shape[1]
    q_per_kv = dq // (n_kv * dh)
    nb = seq // WINDOW
    kvw = kv.shape[1]
    grid_spec = pltpu.PrefetchScalarGridSpec(
        num_scalar_prefetch=1, grid=(n_batch, nb),
        in_specs=[pl.BlockSpec((WINDOW, dq), lambda n, b, s: (n * nb + b, 0)),
                  pl.BlockSpec((WINDOW, kvw), lambda n, b, s: (n * nb + jnp.maximum(b - 1, 0), 0)),
                  pl.BlockSpec((WINDOW, kvw), lambda n, b, s: (n * nb + b, 0))],
        out_specs=pl.BlockSpec((WINDOW, dq), lambda n, b, s: (n * nb + b, 0)))
    return pl.pallas_call(
        functools.partial(_win_prompt_body, n_kv=n_kv, q_per_kv=q_per_kv, dh=dh, sink_row=jl),
        grid_spec=grid_spec,
        out_shape=jax.ShapeDtypeStruct((q.shape[0], d_out), BF16),
        compiler_params=_cparams(2), name="win_attn_prompt",
    )(sinks, q, kv, kv)


def _win_sample_body(q_ref, k_ref, v_ref, sink_ref, o_ref, *, n_kv, seq, n_buf):
    rows = q_ref.shape[2]
    nk = k_ref.shape[2]
    assert seq & (seq - 1) == 0
    t = lax.broadcasted_iota(I32, (rows, nk), 0) & (seq - 1)
    kj = lax.broadcasted_iota(I32, (rows, nk), 1)
    diff = t - (kj - n_buf)
    mask = (diff >= 0) & (diff < WINDOW)
    dh = q_ref.shape[3]
    for h in range(n_kv):
        s = lax.dot_general(q_ref[0, h].astype(BF16), k_ref[0, h].astype(BF16),
                            (((1,), (1,)), ((), ())), preferred_element_type=F32) * dh ** -0.5
        o_ref[0, h] = _sink_attend(s, mask, sink_ref[0, h], v_ref[0, h].astype(BF16))


def _win_attn_sample(q, kk, vv, sink_rows, seq, n_buf):
    n, n_kv, rows, dh = q.shape
    nk = kk.shape[2]
    blk = lambda r: pl.BlockSpec((1, n_kv, r, dh), lambda i: (i, 0, 0, 0))
    return pl.pallas_call(
        functools.partial(_win_sample_body, n_kv=n_kv, seq=seq, n_buf=n_buf),
        grid=(n,),
        in_specs=[blk(rows), blk(nk), blk(nk),
                  pl.BlockSpec((1, n_kv, rows, 1), lambda i: (0, 0, 0, 0))],
        out_specs=blk(rows),
        out_shape=jax.ShapeDtypeStruct((n, n_kv, rows, dh), F32),
        compiler_params=_cparams(), name="win_attn_sample",
    )(q, kk, vv, sink_rows)


def kernel(x_prompt, x_sample, mem_prompt, cache_mem_k, cache_mem_v, cache_win_k, cache_win_v, w_in_a, sgu_g, sgu_b, w_spatial, b_spatial, w_in_b, w_kv, sinks, w_mem_kv, w_out, ln_g, ln_b, router_w, router_b, w_gate_up, w_down, shared_gate_up, shared_down):
    nb, seq, d = x_prompt.shape
    ns, sseq, _ = x_sample.shape
    depth = w_out.shape[0]
    n_a = w_in_a.shape[0]
    mem_len = mem_prompt.shape[1]
    mem_heads, mem_dh = cache_mem_k.shape[3], cache_mem_k.shape[4]
    d_mem = mem_heads * mem_dh
    d_tok = d - d_mem
    n_buf, n_kv, dh = cache_win_k.shape[1], cache_win_k.shape[2], cache_win_k.shape[3]
    kvw = n_kv * dh
    q_per_kv = d_tok // kvw
    chunk = w_spatial.shape[2]
    n_groups = w_spatial.shape[1]
    alpha = float((2 * depth) ** 0.25)
    mp, ms = nb * seq, ns * sseq
    m = mp + ms
    p = dict(router_w=router_w, router_b=router_b, w_gate_up=w_gate_up, w_down=w_down,
             shared_gate_up=shared_gate_up, shared_down=shared_down)

    memb = mem_prompt.reshape(nb * mem_len, d).astype(BF16)
    tm_mem = _pick(nb * mem_len, (512, 256, 128))
    tn_mem = _pick(d_mem, (512, 256, 128))
    n_j, n_r = d_mem // tn_mem, nb * mem_len // tm_mem
    lay = np.repeat(np.arange(depth), n_j * n_r)
    jj = np.tile(np.repeat(np.arange(n_j), n_r), depth)
    rr = np.tile(np.arange(n_r), depth * n_j)
    mem_tables = tuple(jnp.asarray(c, I32) for c in
                       (lay, jj, rr, lay, (rr == 0).astype(np.int64), np.ones_like(lay)))
    mem_kw = dict(tm=tm_mem, tn=tn_mem, n_cols=d_mem, out_dtype=F32, n_out=depth)
    pmem_k = _gmm(mem_tables, memb, w_mem_kv, name="mem_k_proj", col0=0, **mem_kw)
    pmem_v = _gmm(mem_tables, memb, w_mem_kv, name="mem_v_proj", col0=d_mem, **mem_kw)
    pmem_k = pmem_k.reshape(depth, nb, mem_len, d_mem)
    pmem_v = pmem_v.reshape(depth, nb, mem_len, d_mem)
    smem_k = cache_mem_k.reshape(depth, ns, mem_len, d_mem)
    smem_v = cache_mem_v.reshape(depth, ns, mem_len, d_mem)

    xf = jnp.concatenate([x_prompt.reshape(mp, d), x_sample.reshape(ms, d)], axis=0)
    xb = xf.astype(BF16)

    tril = jnp.tril(jnp.ones((chunk, chunk), F32))
    c_s = min(sseq, chunk)
    reps = chunk // c_s
    eye = jnp.eye(reps, dtype=F32)
    w_p = w_spatial * tril
    w_c = (w_spatial[:, :, :c_s, :c_s] * tril[:c_s, :c_s])
    w_s = jnp.einsum("ab,lgts->lgatbs", eye, w_c).reshape(n_a, n_groups, chunk, chunk)
    wmix = jnp.stack([w_p, w_s], axis=1)
    bias_p = jnp.swapaxes(b_spatial, 1, 2)
    bias_s = jnp.tile(bias_p[:, :c_s], (1, reps, 1))
    bias = jnp.stack([bias_p, bias_s], axis=1)

    chunk_v = []
    kv = k_new = v_new = kk = vv = None
    for l in range(depth):
        if l < n_a:
            hu = _dense(xb, w_in_a, l, name="in_a_u", n_cols=d_tok, col0=0, act="gelu", out_dtype=BF16)
            hv = _dense(xb, w_in_a, l, name="in_a_v", n_cols=d_tok, col0=d_tok, act="gelu", out_dtype=F32)
            qm = _dense(xb, w_in_a, l, name="in_a_qmem", n_cols=d_mem, col0=2 * d_tok, out_dtype=BF16)
            cat, v_s = _sgu(hu, hv, sgu_g[l], sgu_b[l], wmix[l], bias[l], mp // chunk, d)
            chunk_v.append(v_s.reshape(ns, sseq, d_tok))
        else:
            jl = l - n_a
            if kv is None:
                kv = _dense(xb, w_kv.reshape(1, d, 2 * kvw), 0, name="kv_proj", n_cols=2 * kvw,
                            out_dtype=F32, tn=_pick(2 * kvw, (384, 256, 128)))
                k_new = kv[:, :kvw]
                v_new = kv[:, kvw:]
                kk = jnp.concatenate([cache_win_k, k_new[mp:].reshape(ns, sseq, n_kv, dh)], axis=1)
                vv = jnp.concatenate([cache_win_v, v_new[mp:].reshape(ns, sseq, n_kv, dh)], axis=1)
            q = _dense(xb, w_in_b, jl, name="in_b_q", n_cols=d_tok, col0=0, out_dtype=BF16)
            qm = _dense(xb, w_in_b, jl, name="in_b_qmem", n_cols=d_mem, col0=d_tok, out_dtype=BF16)
            cat = _win_attn_prompt(q, kv, sinks, jl, nb, seq, n_kv, dh, d)
            q_s = q[mp:].astype(F32).reshape(ns, sseq, n_kv, q_per_kv, dh)
            q_s = q_s.transpose(0, 2, 3, 1, 4).reshape(ns, n_kv, q_per_kv * sseq, dh)
            sink_rows = jnp.repeat(sinks[jl].reshape(1, n_kv, q_per_kv), sseq, axis=2)[..., None]
            o_s = _win_attn_sample(q_s, kk.transpose(0, 2, 1, 3), vv.transpose(0, 2, 1, 3),
                                   sink_rows, sseq, n_buf)
            tok_s = o_s.reshape(ns, n_kv, q_per_kv, sseq, dh).transpose(0, 3, 1, 2, 4)
            cat = lax.dynamic_update_slice(cat, tok_s.reshape(ms, d_tok).astype(BF16), (mp, 0))
        cat = _mem_attn_prompt(qm, pmem_k, pmem_v, cat, l, mem_heads, nb, seq)
        cat = _mem_attn_sample(qm, smem_k, smem_v, cat, l, mem_heads, mp, sseq)
        mix = _dense(cat, w_out, l, name="out_proj", n_cols=d, out_dtype=F32)
        xf, xb, x_slabs = _resid_ln(xf, mix, ln_g[l, 0], ln_b[l, 0], alpha)
        xf, xb = _moe_ffn(xf, xb, x_slabs, l, p, ln_g[l, 1], ln_b[l, 1], alpha)

    y_prompt = xf[:mp].reshape(nb, seq, d)
    y_sample = xf[mp:].reshape(ns, sseq, d)
    mshape = (depth, nb, mem_len, mem_heads, mem_dh)
    keep_p = min(WINDOW, seq)
    k_p = k_new[:mp].reshape(nb, seq, n_kv, dh)[:, -keep_p:]
    v_p = v_new[:mp].reshape(nb, seq, n_kv, dh)[:, -keep_p:]
    return (y_prompt, y_sample, pmem_k.reshape(mshape), pmem_v.reshape(mshape), k_p, v_p,
            kk[:, -n_buf:], vv[:, -n_buf:], jnp.stack(chunk_v))
```

```python
import functools

import numpy as np
import jax
import jax.numpy as jnp
from jax import lax
from jax.experimental import pallas as pl
from jax.experimental.pallas import tpu as pltpu

F32, BF16, I32, U32 = jnp.float32, jnp.bfloat16, jnp.int32, jnp.uint32

LANES = 128
TOP_K = 8
N_GROUPS = 8
TOPK_GROUPS = 4
ROUTED_SCALE = 2.5
WINDOW = 128
LN_EPS = 1e-5
MOE_BM = 256
VMEM_LIMIT = 56 * 1024 * 1024


def _pick(n, prefs):
    for p in prefs:
        if n % p == 0:
            return p
    return n


def _cparams(n_axes=1):
    return pltpu.CompilerParams(dimension_semantics=("arbitrary",) * n_axes,
                                vmem_limit_bytes=VMEM_LIMIT)


def _layer_norm(xf, g, b):
    mu = jnp.mean(xf, axis=-1, keepdims=True)
    var = jnp.mean(jnp.square(xf - mu), axis=-1, keepdims=True)
    return (xf - mu) * lax.rsqrt(var + LN_EPS) * g + b


def _pack_rows(a):
    half = a.shape[1] // 2
    lo = pltpu.bitcast(a[:, :half].astype(BF16).astype(F32), U32)
    hi = pltpu.bitcast(a[:, half:].astype(BF16).astype(F32), U32)
    return (lo >> 16) | (hi & jnp.uint32(0xFFFF0000))


def _store_slabs(o_ref, u):
    t = u.shape[0]
    s = u.shape[1] // LANES
    for c in range(s):
        o_ref[pl.ds(c, t, stride=s), :] = u[:, c * LANES:(c + 1) * LANES]


def _unpack_words(u):
    lo = pltpu.bitcast(u << 16, F32)
    hi = pltpu.bitcast(u & jnp.uint32(0xFFFF0000), F32)
    return lo, hi


def _cast_block(src_ref, dst_ref, col0):
    k, tn = src_ref.shape[1], src_ref.shape[2]
    ck = _pick(k, (512, 256, 128))

    def body(c, carry):
        rows = pl.ds(pl.multiple_of(c * ck, ck), ck)
        dst_ref[rows, col0:col0 + tn] = src_ref[0, rows, :].astype(BF16)
        return carry

    lax.fori_loop(0, k // ck, body, 0)


def _gmm_body(e_ref, j_ref, r_ref, eo_ref, first_ref, valid_ref, x_ref, *rest, gated, act, slabs):
    del e_ref, j_ref, r_ref, eo_ref
    if gated:
        w_ref, w2_ref, o_ref, wb_ref = rest
    else:
        w_ref, o_ref, wb_ref = rest
    tn = w_ref.shape[2]
    i = pl.program_id(0)

    @pl.when(first_ref[i] == 1)
    def _cast():
        _cast_block(w_ref, wb_ref, 0)
        if gated:
            _cast_block(w2_ref, wb_ref, tn)

    @pl.when(valid_ref[i] == 1)
    def _compute():
        a = jnp.dot(x_ref[...], wb_ref[...], preferred_element_type=F32)
        if gated:
            a = jax.nn.silu(a[:, :tn]) * a[:, tn:]
        elif act == "gelu":
            a = jax.nn.gelu(a)
        if slabs:
            _store_slabs(o_ref.at[0], _pack_rows(a))
        else:
            o_ref[0] = a.astype(o_ref.dtype)


def _gmm(tables, x, w, *, name, tm, tn, n_cols, col0=0, col0_up=None, act=None, out_dtype=F32,
         n_out=1, e0=0, slabs=False):
    m, k = x.shape
    gated = col0_up is not None
    n_items = tables[0].shape[0]
    assert m % tm == 0 and n_cols % tn == 0 and col0 % tn == 0
    if slabs:
        assert tn == n_cols and n_cols % (2 * LANES) == 0
        spr = n_cols // (2 * LANES)
        out_spec = pl.BlockSpec((1, tm * spr, LANES), lambda i, e, j, r, eo, f, v: (eo[i], r[i], 0))
        out_shape = jax.ShapeDtypeStruct((n_out, m * spr, LANES), U32)
    else:
        out_spec = pl.BlockSpec((1, tm, tn), lambda i, e, j, r, eo, f, v: (eo[i], r[i], j[i]))
        out_shape = jax.ShapeDtypeStruct((n_out, m, n_cols), out_dtype)
    jb0 = col0 // tn
    in_specs = [
        pl.BlockSpec((tm, k), lambda i, e, j, r, eo, f, v: (r[i], 0)),
        pl.BlockSpec((1, k, tn), lambda i, e, j, r, eo, f, v: (e0 + e[i], 0, jb0 + j[i])),
    ]
    operands = [x, w]
    if gated:
        assert col0_up % tn == 0 and tn % LANES == 0
        jb1 = col0_up // tn
        in_specs.append(pl.BlockSpec((1, k, tn), lambda i, e, j, r, eo, f, v: (e0 + e[i], 0, jb1 + j[i])))
        operands.append(w)
    scratch = [pltpu.VMEM((k, 2 * tn if gated else tn), BF16)]
    grid_spec = pltpu.PrefetchScalarGridSpec(
        num_scalar_prefetch=6, grid=(n_items,), in_specs=in_specs, out_specs=out_spec,
        scratch_shapes=scratch)
    return pl.pallas_call(
        functools.partial(_gmm_body, gated=gated, act=act, slabs=slabs),
        grid_spec=grid_spec, out_shape=out_shape, compiler_params=_cparams(), name=name,
    )(*tables, *operands)


def _dense_tables(n_j, n_r, e=0):
    j = np.repeat(np.arange(n_j), n_r)
    r = np.tile(np.arange(n_r), n_j)
    z = np.zeros_like(j)
    cols = (z + e, j, r, z, (r == 0).astype(np.int64), z + 1)
    return tuple(jnp.asarray(c, I32) for c in cols)


def _dense(x, w, e, *, name, n_cols, col0=0, col0_up=None, act=None, out_dtype=F32, tm=None, tn=None):
    m, k = x.shape
    tn = tn or _pick(n_cols, (512, 384, 256, 128))
    n_w = 1 if col0_up is None else 2

    def vmem_bytes(t):
        return 2 * t * k * 2 + n_w * k * tn * (2 * 4 + 2) + 2 * t * tn * jnp.dtype(out_dtype).itemsize

    fits = [t for t in (1408, 768, 512, 256, 128) if m % t == 0 and vmem_bytes(t) <= VMEM_LIMIT - (6 << 20)]
    tm = tm or (fits[0] if fits else m)
    tables = _dense_tables(n_cols // tn, m // tm, e)
    return _gmm(tables, x, w, name=name, tm=tm, tn=tn, n_cols=n_cols, col0=col0, col0_up=col0_up,
                act=act, out_dtype=out_dtype)[0]


def _resid_ln_body(x_ref, y_ref, g_ref, b_ref, o_ref, ob_ref, os_ref, *, alpha):
    o = _layer_norm(alpha * x_ref[...] + y_ref[...], g_ref[...], b_ref[...])
    o_ref[...] = o
    ob_ref[...] = o.astype(BF16)
    _store_slabs(os_ref, _pack_rows(o))


def _resid_ln(x, y, g, b, alpha):
    m, d = x.shape
    tm = _pick(m, (256, 128))
    spr = d // (2 * LANES)
    row = pl.BlockSpec((tm, d), lambda i: (i, 0))
    vec = pl.BlockSpec((1, d), lambda i: (0, 0))
    return pl.pallas_call(
        functools.partial(_resid_ln_body, alpha=alpha),
        grid=(m // tm,), in_specs=[row, row, vec, vec],
        out_specs=[row, row, pl.BlockSpec((tm * spr, LANES), lambda i: (i, 0))],
        out_shape=[jax.ShapeDtypeStruct((m, d), F32), jax.ShapeDtypeStruct((m, d), BF16),
                   jax.ShapeDtypeStruct((m * spr, LANES), U32)],
        compiler_params=_cparams(), name="resid_ln",
    )(x, y, g.reshape(1, d), b.reshape(1, d))


def _router_body(x_ref, rwh_ref, rwl_ref, rb_ref, eid_ref, pos_ref, gate_ref, cnt_ref, base_ref, *,
                 n_exp):
    tm = x_ref.shape[0]
    gsz = n_exp // N_GROUPS
    i = pl.program_id(0)

    @pl.when(i == 0)
    def _init():
        base_ref[...] = jnp.zeros_like(base_ref)

    x = x_ref[...]
    xh = x.astype(BF16)
    xl = (x - xh.astype(F32)).astype(BF16)
    wh = rwh_ref[...]
    logits = (jnp.dot(xh, wh, preferred_element_type=F32)
              + (jnp.dot(xl, wh, preferred_element_type=F32)
                 + jnp.dot(xh, rwl_ref[...], preferred_element_type=F32)))
    s = jax.nn.sigmoid(logits.T[:n_exp])
    sb = s + rb_ref[...]
    row = lax.broadcasted_iota(I32, (n_exp, tm), 0)
    assert gsz & (gsz - 1) == 0
    grp = lax.shift_right_logical(row, gsz.bit_length() - 1)
    bcast = lambda a, e2: jnp.broadcast_to(a[e2:e2 + 1, :], (n_exp, tm))

    def count_beaten(a, idx, same=None):
        n = jnp.zeros((n_exp, tm), I32)
        for e2 in range(n_exp):
            c = bcast(a, e2)
            beats = (c > a) | ((c == a) & (e2 < idx))
            if same is not None:
                beats = beats & (same == e2 // gsz)
            n = n + jnp.where(beats, 1, 0)
        return n

    top2 = jnp.where(count_beaten(sb, row, grp) < 2, sb, 0.0)
    gsum = jnp.zeros((n_exp, tm), F32)
    for e2 in range(n_exp):
        gsum = gsum + jnp.where(grp == e2 // gsz, bcast(top2, e2), 0.0)
    n_beat = jnp.zeros((n_exp, tm), I32)
    for g2 in range(N_GROUPS):
        c = bcast(gsum, g2 * gsz)
        n_beat = n_beat + jnp.where((c > gsum) | ((c == gsum) & (g2 < grp)), 1, 0)
    v = jnp.where(n_beat < TOPK_GROUPS, sb, -jnp.inf)
    sel = count_beaten(v, row) < TOP_K
    w = jnp.where(sel, s, 0.0)
    gate = w / jnp.sum(w, axis=0, keepdims=True) * ROUTED_SCALE

    sel_b = jnp.where(sel, 1.0, 0.0).astype(BF16)
    ti = lax.broadcasted_iota(I32, (tm, tm), 0)
    tj = lax.broadcasted_iota(I32, (tm, tm), 1)
    earlier = jnp.where(ti < tj, 1.0, 0.0).astype(BF16)
    pos = base_ref[:, 0:1] + jnp.dot(sel_b, earlier, preferred_element_type=F32)
    base_ref[...] = base_ref[...] + jnp.sum(sel_b.astype(F32), axis=1, keepdims=True)
    cnt_ref[...] = base_ref[...].astype(I32)

    ei = lax.broadcasted_iota(I32, (n_exp, n_exp), 0)
    ej = lax.broadcasted_iota(I32, (n_exp, n_exp), 1)
    upto = jnp.where(ej <= ei, 1.0, 0.0).astype(BF16)
    slot = jnp.dot(upto, sel_b, preferred_element_type=F32)
    row_f = row.astype(F32)
    for k in range(TOP_K):
        m = sel & (slot == float(k + 1))
        eid_ref[k:k + 1, :] = jnp.sum(jnp.where(m, row_f, 0.0), axis=0, keepdims=True).astype(I32)
        pos_ref[k:k + 1, :] = jnp.sum(jnp.where(m, pos, 0.0), axis=0, keepdims=True).astype(I32)
        gate_ref[k:k + 1, :] = jnp.sum(jnp.where(m, gate, 0.0), axis=0, keepdims=True)


def _router(x, rw, rb):
    m, d = x.shape
    n_exp = rw.shape[1]
    n_pad = -(-n_exp // LANES) * LANES
    tm = _pick(m, (256, 128))
    slot = pl.BlockSpec((TOP_K, tm), lambda i: (0, i))
    rw = jnp.pad(rw, ((0, 0), (0, n_pad - n_exp)))
    rw_hi = rw.astype(BF16)
    rw_lo = (rw - rw_hi.astype(F32)).astype(BF16)
    wspec = pl.BlockSpec((d, n_pad), lambda i: (0, 0))
    eid, pos, gate, cnt = pl.pallas_call(
        functools.partial(_router_body, n_exp=n_exp),
        grid=(m // tm,),
        in_specs=[pl.BlockSpec((tm, d), lambda i: (i, 0)), wspec, wspec,
                  pl.BlockSpec((n_exp, 1), lambda i: (0, 0))],
        out_specs=[slot, slot, slot, pl.BlockSpec((n_exp, LANES), lambda i: (0, 0))],
        out_shape=[jax.ShapeDtypeStruct((TOP_K, m), I32), jax.ShapeDtypeStruct((TOP_K, m), I32),
                   jax.ShapeDtypeStruct((TOP_K, m), F32), jax.ShapeDtypeStruct((n_exp, LANES), I32)],
        scratch_shapes=[pltpu.VMEM((n_exp, LANES), F32)],
        compiler_params=_cparams(), name="router",
    )(x, rw_hi, rw_lo, rb.reshape(n_exp, 1))
    return eid.T, pos.T, gate.T, cnt[:, 0]


def _gather_body(nblk_ref, tok_ref, nxt_ref, x_hbm, o_ref, buf, sem):
    i = pl.program_id(0)
    nblk = nblk_ref[0]
    spr = x_hbm.shape[1]
    bm = buf.shape[0] // (2 * spr)
    slot = lax.rem(i, 2)

    def row_copy(ids_ref, s, r):
        dst = buf.at[pl.ds(pl.multiple_of((s * bm + r) * spr, spr), spr), :]
        return pltpu.make_async_copy(x_hbm.at[ids_ref[0, 0, r]], dst, sem.at[s])

    def start_all(ids_ref, s):
        def body(r, c):
            row_copy(ids_ref, s, r).start()
            return c

        lax.fori_loop(0, bm, body, 0, unroll=8)

    @pl.when(i == 0)
    def _first():
        start_all(tok_ref, 0)

    @pl.when(i + 1 < nblk)
    def _prefetch():
        start_all(nxt_ref, 1 - slot)

    @pl.when(i < nblk)
    def _block():
        def wait(r, c):
            row_copy(tok_ref, slot, r).wait()
            return c

        lax.fori_loop(0, bm, wait, 0, unroll=8)
        base = slot * (bm * spr)
        los, his = [], []
        for c in range(spr):
            lo, hi = _unpack_words(buf[pl.ds(base + c, bm, stride=spr), :])
            los.append(lo.astype(BF16))
            his.append(hi.astype(BF16))
        o_ref[...] = jnp.concatenate(los + his, axis=1)


def _gather_rows(x_slabs, row_tok, n_blocks, bm):
    nb_max = row_tok.shape[0] // bm
    _, spr, _ = x_slabs.shape
    d = spr * 2 * LANES
    ids = row_tok.reshape(nb_max, 1, bm)
    grid_spec = pltpu.PrefetchScalarGridSpec(
        num_scalar_prefetch=1, grid=(nb_max,),
        in_specs=[pl.BlockSpec((1, 1, bm), lambda i, nb: (jnp.minimum(i, nb[0] - 1), 0, 0),
                               memory_space=pltpu.SMEM),
                  pl.BlockSpec((1, 1, bm), lambda i, nb: (jnp.minimum(i + 1, nb[0] - 1), 0, 0),
                               memory_space=pltpu.SMEM),
                  pl.BlockSpec(memory_space=pl.ANY)],
        out_specs=pl.BlockSpec((bm, d), lambda i, nb: (jnp.minimum(i, nb[0] - 1), 0)),
        scratch_shapes=[pltpu.VMEM((2 * bm * spr, LANES), U32), pltpu.SemaphoreType.DMA((2,))])
    return pl.pallas_call(
        _gather_body, grid_spec=grid_spec,
        out_shape=jax.ShapeDtypeStruct((nb_max * bm, d), BF16),
        compiler_params=_cparams(), name="moe_gather",
    )(n_blocks.reshape(1), ids, ids, x_slabs)


def _combine_body(dest_ref, nxt_ref, gate_ref, x_ref, sh_ref, g_ref, b_ref, y_hbm, o_ref, ob_ref,
                  buf, sem, *, alpha):
    i = pl.program_id(0)
    tt = x_ref.shape[0]
    spr = y_hbm.shape[1]
    slot = lax.rem(i, 2)

    def row_copy(ids_ref, s, t, k):
        off = pl.multiple_of(((s * TOP_K + k) * tt + t) * spr, spr)
        return pltpu.make_async_copy(y_hbm.at[ids_ref[0, 0, t * TOP_K + k]],
                                     buf.at[pl.ds(off, spr), :], sem.at[s])

    def start_all(ids_ref, s):
        def body(t, c):
            for k in range(TOP_K):
                row_copy(ids_ref, s, t, k).start()
            return c

        lax.fori_loop(0, tt, body, 0)

    @pl.when(i == 0)
    def _first():
        start_all(dest_ref, 0)

    @pl.when(i + 1 < pl.num_programs(0))
    def _prefetch():
        start_all(nxt_ref, 1 - slot)

    acc = alpha * x_ref[...] + sh_ref[...]

    def wait(t, c):
        for k in range(TOP_K):
            row_copy(dest_ref, slot, t, k).wait()
        return c

    lax.fori_loop(0, tt, wait, 0)
    los, his = [None] * spr, [None] * spr
    for k in range(TOP_K):
        gk = gate_ref[:, k:k + 1]
        base = (slot * TOP_K + k) * (tt * spr)
        for c in range(spr):
            lo, hi = _unpack_words(buf[pl.ds(base + c, tt, stride=spr), :])
            los[c] = gk * lo if k == 0 else los[c] + gk * lo
            his[c] = gk * hi if k == 0 else his[c] + gk * hi
    routed = jnp.concatenate(los + his, axis=1)
    o = _layer_norm(acc + routed, g_ref[...], b_ref[...])
    o_ref[...] = o
    ob_ref[...] = o.astype(BF16)


def _combine_ln(dest, gate, x, shared, y_slabs, g, b, alpha):
    m, d = x.shape
    spr = y_slabs.shape[1]
    tt = _pick(m, (64, 32, 16, 8))
    n = m // tt
    row = pl.BlockSpec((tt, d), lambda i: (i, 0))
    vec = pl.BlockSpec((1, d), lambda i: (0, 0))
    ids = dest.reshape(n, 1, tt * TOP_K)
    return pl.pallas_call(
        functools.partial(_combine_body, alpha=alpha),
        grid=(n,),
        in_specs=[pl.BlockSpec((1, 1, tt * TOP_K), lambda i: (i, 0, 0), memory_space=pltpu.SMEM),
                  pl.BlockSpec((1, 1, tt * TOP_K), lambda i: (jnp.minimum(i + 1, n - 1), 0, 0),
                               memory_space=pltpu.SMEM),
                  pl.BlockSpec((tt, TOP_K), lambda i: (i, 0)), row, row, vec, vec,
                  pl.BlockSpec(memory_space=pl.ANY)],
        out_specs=[row, row],
        out_shape=[jax.ShapeDtypeStruct((m, d), F32), jax.ShapeDtypeStruct((m, d), BF16)],
        scratch_shapes=[pltpu.VMEM((2 * TOP_K * tt * spr, LANES), U32), pltpu.SemaphoreType.DMA((2,))],
        compiler_params=_cparams(), name="moe_combine",
    )(ids, ids, gate, x, shared, g.reshape(1, d), b.reshape(1, d), y_slabs)


def _moe_tables(counts, n_j, nb_max, bm):
    n_exp = counts.shape[0]
    nb = (counts + bm - 1) // bm
    blk_end = jnp.cumsum(nb)
    blk_start = blk_end - nb
    total = blk_end[-1]
    i = jnp.arange(n_j * nb_max, dtype=I32)
    valid = i < n_j * total
    ids = jnp.arange(n_exp, dtype=I32)
    own = (i[:, None] >= n_j * blk_start[None, :]) & (i[:, None] < n_j * blk_end[None, :])
    pick = lambda a: jnp.sum(jnp.where(own, a[None, :], 0), axis=1)
    e, bs, nbe = pick(ids), pick(blk_start), jnp.maximum(pick(nb), 1)
    local = i - n_j * bs
    j = local // nbe
    rb = local - j * nbe
    r = bs + rb
    e_last = jnp.max(jnp.where(nb > 0, ids, 0))
    e = jnp.where(valid, e, e_last)
    j = jnp.where(valid, j, n_j - 1)
    r = jnp.where(valid, r, total - 1)
    first = ((rb == 0) & valid).astype(I32)
    tables = (e, j, r, jnp.zeros_like(i), first, valid)
    return tuple(t.astype(I32) for t in tables), blk_start, total


def _moe_ffn(xf, xb, x_slabs, l, p, ln_g, ln_b, alpha):
    m, d = xf.shape
    spr = d // (2 * LANES)
    n_exp = p["router_w"].shape[-1]
    f = p["w_down"].shape[2]
    bm = MOE_BM
    nb_max = (m * TOP_K) // bm + n_exp
    eid, pos, gate, counts = _router(xf, p["router_w"][l], p["router_b"][l])
    cn = _pick(f, (384, 256, 128))
    up_tables, blk_start, total = _moe_tables(counts, f // cn, nb_max, bm)
    dn_tables, _, _ = _moe_tables(counts, 1, nb_max, bm)
    start_of = jnp.sum(jnp.where(eid[..., None] == jnp.arange(n_exp, dtype=I32), blk_start, 0), axis=-1)
    dest = bm * start_of + pos
    tok = jnp.broadcast_to(jnp.arange(m, dtype=I32)[:, None], (m, TOP_K))
    row_tok = jnp.zeros((nb_max * bm,), I32).at[dest.reshape(-1)].set(
        tok.reshape(-1), unique_indices=True)
    per = 2 if nb_max % 2 == 0 else 1
    xs = _gather_rows(x_slabs.reshape(m, spr, LANES), row_tok,
                      ((total + per - 1) // per).astype(I32), per * bm)
    w_gu = p["w_gate_up"].reshape(-1, d, 2 * f)
    w_dn = p["w_down"].reshape(-1, f, d)
    h = _gmm(up_tables, xs, w_gu, name="moe_up", tm=bm, tn=cn, n_cols=f, col0=0, col0_up=f,
             out_dtype=BF16, e0=l * n_exp)[0]
    y = _gmm(dn_tables, h, w_dn, name="moe_down", tm=bm, tn=d, n_cols=d, e0=l * n_exp, slabs=True)[0]
    y = y.reshape(nb_max * bm, spr, LANES)
    hs = _dense(xb, p["shared_gate_up"], l, name="shared_up", n_cols=f, col0=0, col0_up=f,
                out_dtype=BF16, tn=cn)
    ys = _dense(hs, p["shared_down"], l, name="shared_down", n_cols=d, out_dtype=F32,
                tn=_pick(d, (1024, 512, 256, 128)))
    return _combine_ln(dest, gate, xf, ys, y, ln_g, ln_b, alpha)


def _sgu_body(hu_ref, hv_ref, g_ref, b_ref, wm_ref, bias_ref, o_ref, v_ref, *, n_groups):
    v = _layer_norm(hv_ref[...], g_ref[...], b_ref[...])
    v_ref[...] = v
    vb = v.astype(BF16)
    gd = v.shape[1] // n_groups
    for g in range(n_groups):
        cols = slice(g * gd, (g + 1) * gd)
        mixed = jnp.dot(wm_ref[0, g].astype(BF16), vb[:, cols], preferred_element_type=F32)
        mixed = mixed + bias_ref[0][:, g:g + 1]
        o_ref[:, cols] = (hu_ref[:, cols].astype(F32) * mixed).astype(BF16)


def _sgu(hu, hv, g, b, wmix, bias, n_prompt_blocks, d_out):
    m, dt = hv.shape
    c = wmix.shape[-1]
    ng = wmix.shape[1]
    nblk = m // c
    npb = n_prompt_blocks
    row = pl.BlockSpec((c, dt), lambda i: (i, 0))
    vec = pl.BlockSpec((1, dt), lambda i: (0, 0))
    kind = lambda i: jnp.where(i >= npb, 1, 0)
    return pl.pallas_call(
        functools.partial(_sgu_body, n_groups=ng),
        grid=(nblk,),
        in_specs=[row, row, vec, vec,
                  pl.BlockSpec((1, ng, c, c), lambda i: (kind(i), 0, 0, 0)),
                  pl.BlockSpec((1, c, ng), lambda i: (kind(i), 0, 0))],
        out_specs=[row, pl.BlockSpec((c, dt), lambda i: (jnp.maximum(i - npb, 0), 0))],
        out_shape=[jax.ShapeDtypeStruct((m, d_out), BF16),
                   jax.ShapeDtypeStruct(((nblk - npb) * c, dt), F32)],
        compiler_params=_cparams(), name="spatial_gate",
    )(hu, hv, g.reshape(1, dt), b.reshape(1, dt), wmix, bias)


def _softmax_rows(s):
    mx = jnp.max(s, axis=-1, keepdims=True)
    p = jnp.exp(s - mx)
    return p / jnp.sum(p, axis=-1, keepdims=True)


def _mem_heads(q, k_ref, v_ref, n_heads):
    dh = q.shape[1] // n_heads
    outs = []
    for hh in range(n_heads):
        cols = slice(hh * dh, (hh + 1) * dh)
        s = lax.dot_general(q[:, cols], k_ref[:, cols].astype(BF16), (((1,), (1,)), ((), ())),
                            preferred_element_type=F32) * dh ** -0.5
        p = _softmax_rows(s).astype(BF16)
        outs.append(jnp.dot(p, v_ref[:, cols].astype(BF16), preferred_element_type=F32))
    return jnp.concatenate(outs, axis=1)


def _mem_prompt_body(q_ref, k_ref, v_ref, cat_ref, o_ref, *, n_heads):
    del cat_ref
    o_ref[...] = _mem_heads(q_ref[...], k_ref.at[0, 0], v_ref.at[0, 0], n_heads).astype(BF16)


def _mem_attn_prompt(q, mem_k, mem_v, cat, l, n_heads, n_batch, seq):
    dq = q.shape[1]
    mlen = mem_k.shape[2]
    tq = _pick(seq, (512, 256, 128))
    nt = seq // tq
    cb = cat.shape[1] // dq - 1
    kv = pl.BlockSpec((1, 1, mlen, dq), lambda n, i: (l, n, 0, 0))
    return pl.pallas_call(
        functools.partial(_mem_prompt_body, n_heads=n_heads),
        grid=(n_batch, nt),
        in_specs=[pl.BlockSpec((tq, dq), lambda n, i: (n * nt + i, 0)), kv, kv,
                  pl.BlockSpec(memory_space=pl.ANY)],
        out_specs=pl.BlockSpec((tq, dq), lambda n, i: (n * nt + i, cb)),
        out_shape=jax.ShapeDtypeStruct(cat.shape, cat.dtype),
        input_output_aliases={3: 0},
        compiler_params=_cparams(2), name="mem_attn_prompt",
    )(q, mem_k, mem_v, cat)


def _mem_sample_body(q_ref, k_ref, v_ref, cat_ref, o_ref, *, n_heads, seq):
    del cat_ref
    q = q_ref[...]
    rows = lax.broadcasted_iota(I32, (q.shape[0], 1), 0)
    out = jnp.zeros(q.shape, F32)
    for s in range(q.shape[0] // seq):
        o = _mem_heads(q, k_ref.at[0, s], v_ref.at[0, s], n_heads)
        out = jnp.where((rows >= s * seq) & (rows < (s + 1) * seq), o, out)
    o_ref[...] = out.astype(BF16)


def _mem_attn_sample(q, mem_k, mem_v, cat, l, n_heads, row0, seq):
    dq = q.shape[1]
    n_seq, mlen = mem_k.shape[1], mem_k.shape[2]
    per = max(16 // seq, 1)
    tq = per * seq
    assert row0 % tq == 0 and n_seq % per == 0
    cb = cat.shape[1] // dq - 1
    kv = pl.BlockSpec((1, per, mlen, dq), lambda i: (l, i, 0, 0))
    return pl.pallas_call(
        functools.partial(_mem_sample_body, n_heads=n_heads, seq=seq),
        grid=(n_seq // per,),
        in_specs=[pl.BlockSpec((tq, dq), lambda i: (row0 // tq + i, 0)), kv, kv,
                  pl.BlockSpec(memory_space=pl.ANY)],
        out_specs=pl.BlockSpec((tq, dq), lambda i: (row0 // tq + i, cb)),
        out_shape=jax.ShapeDtypeStruct(cat.shape, cat.dtype),
        input_output_aliases={3: 0},
        compiler_params=_cparams(), name="mem_attn_sample",
    )(q, mem_k, mem_v, cat)


def _sink_attend(s, mask, sink, vb):
    s = jnp.where(mask, s, -jnp.inf)
    mx = jnp.maximum(jnp.max(s, axis=-1, keepdims=True), sink)
    p = jnp.exp(s - mx)
    den = jnp.sum(p, axis=-1, keepdims=True) + jnp.exp(sink - mx)
    return jnp.dot(p.astype(BF16), vb, preferred_element_type=F32) / den


def _win_prompt_body(sink_ref, q_ref, kvp_ref, kvc_ref, o_ref, *, n_kv, q_per_kv, dh, sink_row):
    b = pl.program_id(1)
    w = q_ref.shape[0]
    kvw = n_kv * dh
    qi = lax.broadcasted_iota(I32, (w, 2 * w), 0)
    kj = lax.broadcasted_iota(I32, (w, 2 * w), 1)
    mask = (kj > qi) & (kj <= qi + w) & ((kj >= w) | (b > 0))
    outs = []
    for h in range(n_kv):
        kc = slice(h * dh, (h + 1) * dh)
        vc = slice(kvw + h * dh, kvw + (h + 1) * dh)
        kb = jnp.concatenate([kvp_ref[:, kc], kvc_ref[:, kc]], axis=0).astype(BF16)
        vb = jnp.concatenate([kvp_ref[:, vc], kvc_ref[:, vc]], axis=0).astype(BF16)
        for g in range(q_per_kv):
            hq = h * q_per_kv + g
            s = lax.dot_general(q_ref[:, hq * dh:(hq + 1) * dh], kb, (((1,), (1,)), ((), ())),
                                preferred_element_type=F32) * dh ** -0.5
            outs.append(_sink_attend(s, mask, sink_ref[sink_row, hq], vb))
    o_ref[...] = jnp.concatenate(outs, axis=1).astype(BF16)


def _win_attn_prompt(q, kv, sinks, jl, n_batch, seq, n_kv, dh, d_out):
    dq = q.shape[1]
    q_per_kv = dq // (n_kv * dh)
    nb = seq // WINDOW
    kvw = kv.shape[1]
    grid_spec = pltpu.PrefetchScalarGridSpec(
        num_scalar_prefetch=1, grid=(n_batch, nb),
        in_specs=[pl.BlockSpec((WINDOW, dq), lambda n, b, s: (n * nb + b, 0)),
                  pl.BlockSpec((WINDOW, kvw), lambda n, b, s: (n * nb + jnp.maximum(b - 1, 0), 0)),
                  pl.BlockSpec((WINDOW, kvw), lambda n, b, s: (n * nb + b, 0))],
        out_specs=pl.BlockSpec((WINDOW, dq), lambda n, b, s: (n * nb + b, 0)))
    return pl.pallas_call(
        functools.partial(_win_prompt_body, n_kv=n_kv, q_per_kv=q_per_kv, dh=dh, sink_row=jl),
        grid_spec=grid_spec,
        out_shape=jax.ShapeDtypeStruct((q.shape[0], d_out), BF16),
        compiler_params=_cparams(2), name="win_attn_prompt",
    )(sinks, q, kv, kv)


def _win_sample_body(q_ref, k_ref, v_ref, sink_ref, o_ref, *, n_kv, seq, n_buf):
    rows = q_ref.shape[2]
    nk = k_ref.shape[2]
    assert seq & (seq - 1) == 0
    t = lax.broadcasted_iota(I32, (rows, nk), 0) & (seq - 1)
    kj = lax.broadcasted_iota(I32, (rows, nk), 1)
    diff = t - (kj - n_buf)
    mask = (diff >= 0) & (diff < WINDOW)
    dh = q_ref.shape[3]
    for h in range(n_kv):
        s = lax.dot_general(q_ref[0, h].astype(BF16), k_ref[0, h].astype(BF16),
                            (((1,), (1,)), ((), ())), preferred_element_type=F32) * dh ** -0.5
        o_ref[0, h] = _sink_attend(s, mask, sink_ref[0, h], v_ref[0, h].astype(BF16))


def _win_attn_sample(q, kk, vv, sink_rows, seq, n_buf):
    n, n_kv, rows, dh = q.shape
    nk = kk.shape[2]
    blk = lambda r: pl.BlockSpec((1, n_kv, r, dh), lambda i: (i, 0, 0, 0))
    return pl.pallas_call(
        functools.partial(_win_sample_body, n_kv=n_kv, seq=seq, n_buf=n_buf),
        grid=(n,),
        in_specs=[blk(rows), blk(nk), blk(nk),
                  pl.BlockSpec((1, n_kv, rows, 1), lambda i: (0, 0, 0, 0))],
        out_specs=blk(rows),
        out_shape=jax.ShapeDtypeStruct((n, n_kv, rows, dh), F32),
        compiler_params=_cparams(), name="win_attn_sample",
    )(q, kk, vv, sink_rows)


def kernel(x_prompt, x_sample, mem_prompt, cache_mem_k, cache_mem_v, cache_win_k, cache_win_v, w_in_a, sgu_g, sgu_b, w_spatial, b_spatial, w_in_b, w_kv, sinks, w_mem_kv, w_out, ln_g, ln_b, router_w, router_b, w_gate_up, w_down, shared_gate_up, shared_down):
    nb, seq, d = x_prompt.shape
    ns, sseq, _ = x_sample.shape
    depth = w_out.shape[0]
    n_a = w_in_a.shape[0]
    mem_len = mem_prompt.shape[1]
    mem_heads, mem_dh = cache_mem_k.shape[3], cache_mem_k.shape[4]
    d_mem = mem_heads * mem_dh
    d_tok = d - d_mem
    n_buf, n_kv, dh = cache_win_k.shape[1], cache_win_k.shape[2], cache_win_k.shape[3]
    kvw = n_kv * dh
    q_per_kv = d_tok // kvw
    chunk = w_spatial.shape[2]
    n_groups = w_spatial.shape[1]
    alpha = float((2 * depth) ** 0.25)
    mp, ms = nb * seq, ns * sseq
    m = mp + ms
    p = dict(router_w=router_w, router_b=router_b, w_gate_up=w_gate_up, w_down=w_down,
             shared_gate_up=shared_gate_up, shared_down=shared_down)

    memb = mem_prompt.reshape(nb * mem_len, d).astype(BF16)
    tm_mem = _pick(nb * mem_len, (512, 256, 128))
    tn_mem = _pick(d_mem, (512, 256, 128))
    n_j, n_r = d_mem // tn_mem, nb * mem_len // tm_mem
    lay = np.repeat(np.arange(depth), n_j * n_r)
    jj = np.tile(np.repeat(np.arange(n_j), n_r), depth)
    rr = np.tile(np.arange(n_r), depth * n_j)
    mem_tables = tuple(jnp.asarray(c, I32) for c in
                       (lay, jj, rr, lay, (rr == 0).astype(np.int64), np.ones_like(lay)))
    mem_kw = dict(tm=tm_mem, tn=tn_mem, n_cols=d_mem, out_dtype=F32, n_out=depth)
    pmem_k = _gmm(mem_tables, memb, w_mem_kv, name="mem_k_proj", col0=0, **mem_kw)
    pmem_v = _gmm(mem_tables, memb, w_mem_kv, name="mem_v_proj", col0=d_mem, **mem_kw)
    pmem_k = pmem_k.reshape(depth, nb, mem_len, d_mem)
    pmem_v = pmem_v.reshape(depth, nb, mem_len, d_mem)
    smem_k = cache_mem_k.reshape(depth, ns, mem_len, d_mem)
    smem_v = cache_mem_v.reshape(depth, ns, mem_len, d_mem)

    xf = jnp.concatenate([x_prompt.reshape(mp, d), x_sample.reshape(ms, d)], axis=0)
    xb = xf.astype(BF16)

    tril = jnp.tril(jnp.ones((chunk, chunk), F32))
    c_s = min(sseq, chunk)
    reps = chunk // c_s
    eye = jnp.eye(reps, dtype=F32)
    w_p = w_spatial * tril
    w_c = (w_spatial[:, :, :c_s, :c_s] * tril[:c_s, :c_s])
    w_s = jnp.einsum("ab,lgts->lgatbs", eye, w_c).reshape(n_a, n_groups, chunk, chunk)
    wmix = jnp.stack([w_p, w_s], axis=1)
    bias_p = jnp.swapaxes(b_spatial, 1, 2)
    bias_s = jnp.tile(bias_p[:, :c_s], (1, reps, 1))
    bias = jnp.stack([bias_p, bias_s], axis=1)

    chunk_v = []
    kv = k_new = v_new = kk = vv = None
    for l in range(depth):
        if l < n_a:
            hu = _dense(xb, w_in_a, l, name="in_a_u", n_cols=d_tok, col0=0, act="gelu", out_dtype=BF16)
            hv = _dense(xb, w_in_a, l, name="in_a_v", n_cols=d_tok, col0=d_tok, act="gelu", out_dtype=F32)
            qm = _dense(xb, w_in_a, l, name="in_a_qmem", n_cols=d_mem, col0=2 * d_tok, out_dtype=BF16)
            cat, v_s = _sgu(hu, hv, sgu_g[l], sgu_b[l], wmix[l], bias[l], mp // chunk, d)
            chunk_v.append(v_s.reshape(ns, sseq, d_tok))
        else:
            jl = l - n_a
            if kv is None:
                kv = _dense(xb, w_kv.reshape(1, d, 2 * kvw), 0, name="kv_proj", n_cols=2 * kvw,
                            out_dtype=F32, tn=_pick(2 * kvw, (384, 256, 128)))
                k_new = kv[:, :kvw]
                v_new = kv[:, kvw:]
                kk = jnp.concatenate([cache_win_k, k_new[mp:].reshape(ns, sseq, n_kv, dh)], axis=1)
                vv = jnp.concatenate([cache_win_v, v_new[mp:].reshape(ns, sseq, n_kv, dh)], axis=1)
            q = _dense(xb, w_in_b, jl, name="in_b_q", n_cols=d_tok, col0=0, out_dtype=BF16)
            qm = _dense(xb, w_in_b, jl, name="in_b_qmem", n_cols=d_mem, col0=d_tok, out_dtype=BF16)
            cat = _win_attn_prompt(q, kv, sinks, jl, nb, seq, n_kv, dh, d)
            q_s = q[mp:].astype(F32).reshape(ns, sseq, n_kv, q_per_kv, dh)
            q_s = q_s.transpose(0, 2, 3, 1, 4).reshape(ns, n_kv, q_per_kv * sseq, dh)
            sink_rows = jnp.repeat(sinks[jl].reshape(1, n_kv, q_per_kv), sseq, axis=2)[..., None]
            o_s = _win_attn_sample(q_s, kk.transpose(0, 2, 1, 3), vv.transpose(0, 2, 1, 3),
                                   sink_rows, sseq, n_buf)
            tok_s = o_s.reshape(ns, n_kv, q_per_kv, sseq, dh).transpose(0, 3, 1, 2, 4)
            cat = lax.dynamic_update_slice(cat, tok_s.reshape(ms, d_tok).astype(BF16), (mp, 0))
        cat = _mem_attn_prompt(qm, pmem_k, pmem_v, cat, l, mem_heads, nb, seq)
        cat = _mem_attn_sample(qm, smem_k, smem_v, cat, l, mem_heads, mp, sseq)
        mix = _dense(cat, w_out, l, name="out_proj", n_cols=d, out_dtype=F32)
        xf, xb, x_slabs = _resid_ln(xf, mix, ln_g[l, 0], ln_b[l, 0], alpha)
        xf, xb = _moe_ffn(xf, xb, x_slabs, l, p, ln_g[l, 1], ln_b[l, 1], alpha)

    y_prompt = xf[:mp].reshape(nb, seq, d)
    y_sample = xf[mp:].reshape(ns, sseq, d)
    mshape = (depth, nb, mem_len, mem_heads, mem_dh)
    keep_p = min(WINDOW, seq)
    k_p = k_new[:mp].reshape(nb, seq, n_kv, dh)[:, -keep_p:]
    v_p = v_new[:mp].reshape(nb, seq, n_kv, dh)[:, -keep_p:]
    return (y_prompt, y_sample, pmem_k.reshape(mshape), pmem_v.reshape(mshape), k_p, v_p,
            kk[:, -n_buf:], vv[:, -n_buf:], jnp.stack(chunk_v))
```
